```python
import math
import jax, jax.numpy as jnp
from jax import lax
import numpy as np

D_MODEL = 1024
BATCH = 16
SEQ = 2048
DEPTH = 1

DA_HEADS = D_MODEL // 128
DA_HEAD_DIM = 64
SWA_Q_HEADS = D_MODEL // 64
SWA_KV_HEADS = SWA_Q_HEADS // 4
SWA_HEAD_DIM = 64
WINDOW = 128
Q_BLOCK = 128
D_FF = ((8 * D_MODEL // 3 + 127) // 128) * 128
RMS_EPS = 1e-6

DA_QK_W = DA_HEADS * 2 * DA_HEAD_DIM
DA_V_W = DA_HEADS * 2 * DA_HEAD_DIM
SWA_Q_W = SWA_Q_HEADS * SWA_HEAD_DIM
SWA_KV_W = SWA_KV_HEADS * SWA_HEAD_DIM
IN_COLS = 2 * DA_QK_W + DA_V_W + SWA_Q_W + 2 * SWA_KV_W + 2 * D_MODEL

kernel_name = "hybrid_diffattn_swa_gated_macaron"


def rmsnorm(x, g):
    xf = x.astype(jnp.float32)
    y = xf * lax.rsqrt(jnp.mean(xf * xf, axis=-1, keepdims=True) + RMS_EPS)
    return (y * g.astype(jnp.float32)).astype(x.dtype)


def alibi_slopes(n):
    return 2.0 ** (-8.0 * jnp.arange(1, n + 1, dtype=jnp.float32) / n)


def swiglu(x, w_gate, w_up, w_down):
    return (jax.nn.silu(x @ w_gate) * (x @ w_up)) @ w_down


def diff_attention(q, k, v, lam, lam_init, subnorm_g):
    B, S = q.shape[0], q.shape[1]
    nb = S // Q_BLOCK
    scale = DA_HEAD_DIM ** -0.5
    slopes = alibi_slopes(DA_HEADS)
    kpos = jnp.arange(S)
    qb = q.reshape(B, nb, Q_BLOCK, DA_HEADS, 2, DA_HEAD_DIM).transpose(1, 0, 2, 3, 4, 5)

    def block(args):
        qblk, n = args
        s = jnp.einsum('bqhcd,bkhcd->bhcqk', qblk, k).astype(jnp.float32) * scale
        qpos = n * Q_BLOCK + jnp.arange(Q_BLOCK)
        dist = jnp.abs(qpos[:, None] - kpos[None, :]).astype(jnp.float32)
        s = s - slopes[:, None, None, None] * dist
        p = jax.nn.softmax(s, axis=-1)
        a = p[:, :, 0] - lam * p[:, :, 1]
        return jnp.einsum('bhqk,bkhe->bqhe', a.astype(v.dtype), v)

    o = lax.map(block, (qb, jnp.arange(nb)))
    o = o.transpose(1, 0, 2, 3, 4).reshape(B, S, DA_HEADS, 2 * DA_HEAD_DIM)
    o = rmsnorm(o, subnorm_g) * (1.0 - lam_init)
    return o.reshape(B, S, DA_V_W)


def window_attention(q, k, v, sink):
    B, S = q.shape[0], q.shape[1]
    nb = S // Q_BLOCK
    G = SWA_Q_HEADS // SWA_KV_HEADS
    scale = SWA_HEAD_DIM ** -0.5
    qb = q.reshape(B, nb, Q_BLOCK, SWA_KV_HEADS, G, SWA_HEAD_DIM)

    def band(t):
        tp = jnp.pad(t, ((0, 0), (Q_BLOCK, Q_BLOCK), (0, 0), (0, 0)))
        tp = tp.reshape(B, nb + 2, Q_BLOCK, SWA_KV_HEADS, SWA_HEAD_DIM)
        return jnp.concatenate([tp[:, :-2], tp[:, 1:-1], tp[:, 2:]], axis=2)

    kw, vw = band(k), band(v)
    s = jnp.einsum('bnqhgd,bnkhd->bnhgqk', qb, kw).astype(jnp.float32) * scale
    r = jnp.arange(Q_BLOCK)
    j = jnp.arange(3 * Q_BLOCK)
    dist = jnp.abs(r[:, None] - j[None, :] + Q_BLOCK)
    kpos = jnp.arange(nb)[:, None] * Q_BLOCK - Q_BLOCK + j[None, :]
    valid = (dist <= WINDOW)[None] & ((kpos >= 0) & (kpos < S))[:, None, :]
    slopes = alibi_slopes(SWA_Q_HEADS).reshape(SWA_KV_HEADS, G)
    s = s - slopes[:, :, None, None] * dist.astype(jnp.float32)
    s = jnp.where(valid[:, None, None], s, -jnp.inf)
    sink_l = sink.astype(jnp.float32).reshape(SWA_KV_HEADS, G)[:, :, None, None]
    m = jnp.maximum(jnp.max(s, axis=-1, keepdims=True), sink_l)
    e = jnp.exp(s - m)
    p = e / (jnp.sum(e, axis=-1, keepdims=True) + jnp.exp(sink_l - m))
    o = jnp.einsum('bnhgqk,bnkhd->bnqhgd', p.astype(v.dtype), vw)
    return o.reshape(B, S, SWA_Q_W)


def setup_inputs(seed: int = 0) -> dict:
    key = jax.random.key(seed)
    ks = jax.random.split(key, 24)
    f32 = jnp.float32

    def w(k, shape, fan_in, mult=1.0):
        return jax.random.normal(k, shape, f32) * (mult * fan_in ** -0.5)

    def gain(k, shape):
        return 1.0 + 0.02 * jax.random.normal(k, shape, f32)

    return {
        "x": jax.random.normal(ks[0], (BATCH, SEQ, D_MODEL), f32),
        "norm_ffn1": gain(ks[1], (DEPTH, D_MODEL)),
        "ffn1_gate": w(ks[2], (DEPTH, D_MODEL, D_FF), D_MODEL),
        "ffn1_up": w(ks[3], (DEPTH, D_MODEL, D_FF), D_MODEL),
        "ffn1_down": w(ks[4], (DEPTH, D_FF, D_MODEL), D_FF),
        "norm_mix": gain(ks[5], (DEPTH, D_MODEL)),
        "w_in": w(ks[6], (DEPTH, D_MODEL, IN_COLS), D_MODEL),
        "b_gate": 0.01 * jax.random.normal(ks[7], (DEPTH, 2, D_MODEL), f32),
        "da_lambda": 0.1 * jax.random.normal(ks[8], (DEPTH, 4, DA_HEAD_DIM), f32),
        "da_subnorm": gain(ks[9], (DEPTH, 2 * DA_HEAD_DIM)),
        "swa_sink": 0.5 * jax.random.normal(ks[10], (DEPTH, SWA_Q_HEADS), f32),
        "w_proj_da": w(ks[11], (DEPTH, DA_V_W, D_MODEL), DA_V_W),
        "w_proj_swa": w(ks[12], (DEPTH, SWA_Q_W, D_MODEL), SWA_Q_W),
        "w_out": w(ks[13], (DEPTH, D_MODEL, D_MODEL), D_MODEL),
        "norm_ffn2": gain(ks[14], (DEPTH, D_MODEL)),
        "ffn2_gate": w(ks[15], (DEPTH, D_MODEL, D_FF), D_MODEL),
        "ffn2_up": w(ks[16], (DEPTH, D_MODEL, D_FF), D_MODEL),
        "ffn2_down": w(ks[17], (DEPTH, D_FF, D_MODEL), D_FF),
        "norm_final": gain(ks[18], (D_MODEL,)),
    }


def reference(x, norm_ffn1, ffn1_gate, ffn1_up, ffn1_down, norm_mix, w_in, b_gate,
              da_lambda, da_subnorm, swa_sink, w_proj_da, w_proj_swa, w_out,
              norm_ffn2, ffn2_gate, ffn2_up, ffn2_down, norm_final):
    B, S, _ = x.shape
    splits = np.cumsum([DA_QK_W, DA_QK_W, DA_V_W, SWA_Q_W, SWA_KV_W, SWA_KV_W]).tolist()
    for l in range(DEPTH):
        x = x + 0.5 * swiglu(rmsnorm(x, norm_ffn1[l]), ffn1_gate[l], ffn1_up[l], ffn1_down[l])

        h = rmsnorm(x, norm_mix[l])
        proj = h @ w_in[l]
        da_q, da_k, da_v, sw_q, sw_k, sw_v, gate = jnp.split(proj, splits, axis=-1)

        lam_init = 0.8 - 0.6 * math.exp(-0.3 * l)
        lp = da_lambda[l].astype(jnp.float32)
        lam = jnp.exp(jnp.sum(lp[0] * lp[1])) - jnp.exp(jnp.sum(lp[2] * lp[3])) + lam_init
        o_da = diff_attention(
            da_q.reshape(B, S, DA_HEADS, 2, DA_HEAD_DIM),
            da_k.reshape(B, S, DA_HEADS, 2, DA_HEAD_DIM),
            da_v.reshape(B, S, DA_HEADS, 2 * DA_HEAD_DIM),
            lam, lam_init, da_subnorm[l])
        o_sw = window_attention(
            sw_q.reshape(B, S, SWA_Q_HEADS, SWA_HEAD_DIM),
            sw_k.reshape(B, S, SWA_KV_HEADS, SWA_HEAD_DIM),
            sw_v.reshape(B, S, SWA_KV_HEADS, SWA_HEAD_DIM),
            swa_sink[l])

        g = jax.nn.sigmoid(gate.reshape(B, S, 2, D_MODEL) + b_gate[l])
        merged = g[:, :, 0] * (o_da @ w_proj_da[l]) + g[:, :, 1] * (o_sw @ w_proj_swa[l])
        x = x + merged @ w_out[l]

        x = x + 0.5 * swiglu(rmsnorm(x, norm_ffn2[l]), ffn2_gate[l], ffn2_up[l], ffn2_down[l])
    return rmsnorm(x, norm_final)
```

```python
import functools
import math

import jax
import jax.numpy as jnp
from jax import lax
from jax.experimental import pallas as pl
from jax.experimental.pallas import tpu as pltpu

F32 = jnp.float32
BF16 = jnp.bfloat16

RMS_EPS = 1e-6
LANES = 128
HALF = 64
VMEM_LIMIT = 56 * 1024 * 1024

DA_HEADS = 8
SW_Q_HEADS = 16
SW_KV_HEADS = 4
SW_GROUP = SW_Q_HEADS // SW_KV_HEADS
WINDOW = 128
Q_BLOCK = 128


def _rmsnorm(x, g):
    return x * lax.rsqrt(jnp.mean(x * x, axis=-1, keepdims=True) + RMS_EPS) * g


def _const_spec(shape):
    return pl.BlockSpec(shape, lambda *_: (0,) * len(shape), pipeline_mode=pl.Buffered(1))


def _params(n_axes):
    return pltpu.CompilerParams(
        dimension_semantics=("parallel",) * n_axes, vmem_limit_bytes=VMEM_LIMIT)


def _ffn_kernel(x_ref, g_ref, wg_ref, wu_ref, wd_ref, gf_ref, o_ref, a_scr, *, ff_chunk, final_norm):
    x = x_ref[...]
    hb = _rmsnorm(x, g_ref[...]).astype(BF16)
    d_ff = wg_ref.shape[1]
    for c in range(d_ff // ff_chunk):
        sl = slice(c * ff_chunk, (c + 1) * ff_chunk)
        gate = jnp.dot(hb, wg_ref[:, sl], preferred_element_type=F32)
        up = jnp.dot(hb, wu_ref[:, sl], preferred_element_type=F32)
        a_scr[:, sl] = (gate * jax.nn.sigmoid(gate) * up).astype(BF16)
    y = jnp.dot(a_scr[...], wd_ref[...], preferred_element_type=F32)
    x = x + 0.5 * y
    if final_norm:
        x = _rmsnorm(x, gf_ref[...])
    o_ref[...] = x


def _ffn(x, g, wg, wu, wd, gf, *, final_norm, tm=512, ff_chunk=256):
    T, D = x.shape
    d_ff = wg.shape[1]
    kern = functools.partial(_ffn_kernel, ff_chunk=ff_chunk, final_norm=final_norm)
    return pl.pallas_call(
        kern,
        out_shape=jax.ShapeDtypeStruct((T, D), F32),
        grid=(T // tm,),
        in_specs=[
            pl.BlockSpec((tm, D), lambda i: (i, 0)),
            _const_spec((1, D)),
            _const_spec((D, d_ff)),
            _const_spec((D, d_ff)),
            _const_spec((d_ff, D)),
            _const_spec((1, D)),
        ],
        out_specs=pl.BlockSpec((tm, D), lambda i: (i, 0)),
        scratch_shapes=[pltpu.VMEM((tm, d_ff), BF16)],
        compiler_params=_params(1),
        name="ffn_final" if final_norm else "ffn",
    )(x, g, wg, wu, wd, gf)


def _inproj_kernel(x_ref, g_ref, w_ref, *o_refs, widths, scales):
    hb = _rmsnorm(x_ref[...], g_ref[...]).astype(BF16)
    off = 0
    for o_ref, width, scale in zip(o_refs, widths, scales):
        for c in range(0, width, 512):
            cw = min(512, width - c)
            y = jnp.dot(hb, w_ref[:, off + c:off + c + cw], preferred_element_type=F32)
            if scale != 1.0:
                y = y * scale
            o_ref[:, c:c + cw] = y.astype(BF16)
        off += width


def _inproj(x, g, w, widths, scales, *, tm=512):
    T, D = x.shape
    kern = functools.partial(_inproj_kernel, widths=widths, scales=scales)
    return pl.pallas_call(
        kern,
        out_shape=[jax.ShapeDtypeStruct((T, wd), BF16) for wd in widths],
        grid=(T // tm,),
        in_specs=[
            pl.BlockSpec((tm, D), lambda i: (i, 0)),
            _const_spec((1, D)),
            _const_spec(w.shape),
        ],
        out_specs=[pl.BlockSpec((tm, wd), lambda i: (i, 0)) for wd in widths],
        compiler_params=_params(1),
        name="in_proj",
    )(x, g, w)


def _da_kernel(slopes_ref, q_ref, k_ref, v_ref, lam_ref, sub_ref, o_ref, *, tq, lam_init):
    h = pl.program_id(1)
    qi = pl.program_id(2)
    S = k_ref.shape[0]
    slope = slopes_ref[h]

    lp = lam_ref[...]
    lam = (jnp.exp(jnp.sum(lp[0:1] * lp[1:2], axis=-1, keepdims=True))
           - jnp.exp(jnp.sum(lp[2:3] * lp[3:4], axis=-1, keepdims=True)) + lam_init)

    q = q_ref[...]
    k = k_ref[...]
    v = v_ref[...]
    lane = lax.broadcasted_iota(jnp.int32, q.shape, 1)
    zero = jnp.zeros_like(q)
    dn = (((1,), (1,)), ((), ()))
    s1 = lax.dot_general(jnp.where(lane < HALF, q, zero), k, dn, preferred_element_type=F32)
    s2 = lax.dot_general(jnp.where(lane >= HALF, q, zero), k, dn, preferred_element_type=F32)

    qpos = qi * tq + lax.broadcasted_iota(jnp.int32, (tq, S), 0)
    kpos = lax.broadcasted_iota(jnp.int32, (tq, S), 1)
    bias = slope * jnp.abs(qpos - kpos).astype(F32)

    def softmax_unnorm(s):
        s = s - bias
        e = jnp.exp(s - jnp.max(s, axis=-1, keepdims=True))
        return e, jnp.sum(e, axis=-1, keepdims=True)

    e1, l1 = softmax_unnorm(s1)
    e2, l2 = softmax_unnorm(s2)
    a = e1 * (1.0 / l1) - e2 * (lam / l2)
    o = jnp.dot(a.astype(BF16), v, preferred_element_type=F32)
    o = _rmsnorm(o, sub_ref[...]) * (1.0 - lam_init)
    o_ref[...] = o.astype(BF16)


def _da_attention(q, k, v, da_lambda, subnorm, slopes, *, B, S, lam_init, tq=256):
    T = q.shape[0]
    nq = S // tq
    kern = functools.partial(_da_kernel, tq=tq, lam_init=lam_init)
    return pl.pallas_call(
        kern,
        out_shape=jax.ShapeDtypeStruct((T, DA_HEADS * LANES), BF16),
        grid=(B, DA_HEADS, nq),
        in_specs=[
            pl.BlockSpec(memory_space=pltpu.SMEM),
            pl.BlockSpec((tq, LANES), lambda b, h, i: (b * nq + i, h)),
            pl.BlockSpec((S, LANES), lambda b, h, i: (b, h)),
            pl.BlockSpec((S, LANES), lambda b, h, i: (b, h)),
            _const_spec((4, HALF)),
            _const_spec((1, LANES)),
        ],
        out_specs=pl.BlockSpec((tq, LANES), lambda b, h, i: (b * nq + i, h)),
        compiler_params=_params(3),
        name="da_attn",
    )(slopes, q, k, v, da_lambda, subnorm)


def _sw_kernel(slopes_ref, sink_ref, q_ref, kk_ref, vv_ref, o_ref):
    j = pl.program_id(1)
    n = pl.program_id(2)
    S = kk_ref.shape[0]
    win = 3 * Q_BLOCK
    start = pl.multiple_of(jnp.clip((n - 1) * Q_BLOCK, 0, S - win), Q_BLOCK)
    kwin = kk_ref[pl.ds(start, win), :]
    vwin = vv_ref[pl.ds(start, win), :]

    qpos = n * Q_BLOCK + lax.broadcasted_iota(jnp.int32, (Q_BLOCK, win), 0)
    kpos = start + lax.broadcasted_iota(jnp.int32, (Q_BLOCK, win), 1)
    dist = jnp.abs(qpos - kpos)
    valid = dist <= WINDOW
    distf = dist.astype(F32)
    lane = lax.broadcasted_iota(jnp.int32, (Q_BLOCK, LANES), 1)
    dn = (((1,), (1,)), ((), ()))

    for t in range(SW_GROUP // 2):
        qpair = q_ref[:, t * LANES:(t + 1) * LANES]
        zero = jnp.zeros_like(qpair)
        outs = []
        for half in range(2):
            g = j * SW_GROUP + 2 * t + half
            qm = jnp.where((lane < HALF) if half == 0 else (lane >= HALF), qpair, zero)
            s = lax.dot_general(qm, kwin, dn, preferred_element_type=F32)
            s = jnp.where(valid, s - slopes_ref[g] * distf, -jnp.inf)
            sink = sink_ref[g]
            m = jnp.maximum(jnp.max(s, axis=-1, keepdims=True), sink)
            e = jnp.exp(s - m)
            denom = jnp.sum(e, axis=-1, keepdims=True) + jnp.exp(sink - m)
            p = e * (1.0 / denom)
            outs.append(jnp.dot(p.astype(BF16), vwin, preferred_element_type=F32))
        o_ref[:, t * LANES:(t + 1) * LANES] = jnp.where(lane < HALF, outs[0], outs[1]).astype(BF16)


def _sw_attention(q, kk, vv, slopes, sink, *, B, S):
    T = q.shape[0]
    nb = S // Q_BLOCK
    gw = SW_GROUP * HALF
    return pl.pallas_call(
        _sw_kernel,
        out_shape=jax.ShapeDtypeStruct((T, SW_Q_HEADS * HALF), BF16),
        grid=(B, SW_KV_HEADS, nb),
        in_specs=[
            pl.BlockSpec(memory_space=pltpu.SMEM),
            pl.BlockSpec(memory_space=pltpu.SMEM),
            pl.BlockSpec((Q_BLOCK, gw), lambda b, j, n: (b * nb + n, j)),
            pl.BlockSpec((S, LANES), lambda b, j, n: (b, j)),
            pl.BlockSpec((S, LANES), lambda b, j, n: (b, j)),
        ],
        out_specs=pl.BlockSpec((Q_BLOCK, gw), lambda b, j, n: (b * nb + n, j)),
        compiler_params=_params(3),
        name="sw_attn",
    )(slopes, sink, q, kk, vv)


def _merge_kernel(x_ref, g_ref, wgate_ref, bg_ref, oda_ref, osw_ref, pa_ref, pb_ref, wo_ref, o_ref):
    x = x_ref[...]
    D = x.shape[1]
    hb = _rmsnorm(x, g_ref[...]).astype(BF16)
    gates = jnp.dot(hb, wgate_ref[...], preferred_element_type=F32) + bg_ref[...]
    gates = jax.nn.sigmoid(gates)
    ya = jnp.dot(oda_ref[...], pa_ref[...], preferred_element_type=F32)
    yb = jnp.dot(osw_ref[...], pb_ref[...], preferred_element_type=F32)
    merged = gates[:, :D] * ya + gates[:, D:] * yb
    o_ref[...] = x + jnp.dot(merged.astype(BF16), wo_ref[...], preferred_element_type=F32)


def _merge(x, g, wgate, bg, oda, osw, pa, pb, wo, *, tm=512):
    T, D = x.shape
    row = lambda i: (i, 0)
    return pl.pallas_call(
        _merge_kernel,
        out_shape=jax.ShapeDtypeStruct((T, D), F32),
        grid=(T // tm,),
        in_specs=[
            pl.BlockSpec((tm, D), row),
            _const_spec((1, D)),
            _const_spec((D, 2 * D)),
            _const_spec((1, 2 * D)),
            pl.BlockSpec((tm, D), row),
            pl.BlockSpec((tm, D), row),
            _const_spec((D, D)),
            _const_spec((D, D)),
            _const_spec((D, D)),
        ],
        out_specs=pl.BlockSpec((tm, D), row),
        compiler_params=_params(1),
        name="merge",
    )(x, g, wgate, bg, oda, osw, pa, pb, wo)


def _alibi_slopes(n):
    return 2.0 ** (-8.0 * jnp.arange(1, n + 1, dtype=F32) / n)


def kernel(x, norm_ffn1, ffn1_gate, ffn1_up, ffn1_down, norm_mix, w_in, b_gate, da_lambda, da_subnorm, swa_sink, w_proj_da, w_proj_swa, w_out, norm_ffn2, ffn2_gate, ffn2_up, ffn2_down, norm_final):
    B, S, D = x.shape
    depth = norm_ffn1.shape[0]
    assert depth >= 1, "the final RMSNorm is fused into the last layer's second FFN"
    T = B * S
    xt = x.reshape(T, D)
    bf = lambda a: a.astype(BF16)
    row = lambda a: a.reshape(1, -1)
    gf = row(norm_final)

    da_w = DA_HEADS * LANES
    swq_w = SW_Q_HEADS * HALF
    swkv_w = SW_KV_HEADS * HALF
    o_q, o_k, o_v, o_sq = 0, da_w, 2 * da_w, 3 * da_w
    o_sk = o_sq + swq_w
    o_sv = o_sk + swkv_w
    o_gate = o_sv + swkv_w

    def dup_heads(w):
        w = w.reshape(D, SW_KV_HEADS, 1, HALF)
        return jnp.broadcast_to(w, (D, SW_KV_HEADS, 2, HALF)).reshape(D, SW_KV_HEADS * LANES)

    widths = (da_w, da_w, da_w, swq_w, SW_KV_HEADS * LANES, SW_KV_HEADS * LANES)
    scale = HALF ** -0.5
    scales = (scale, 1.0, 1.0, scale, 1.0, 1.0)

    for l in range(depth):
        lam_init = 0.8 - 0.6 * math.exp(-0.3 * l)
        wl = w_in[l]
        w_qkv = bf(jnp.concatenate(
            [wl[:, o_q:o_sk], dup_heads(wl[:, o_sk:o_sv]), dup_heads(wl[:, o_sv:o_gate])], axis=1))

        xt = _ffn(xt, row(norm_ffn1[l]), bf(ffn1_gate[l]), bf(ffn1_up[l]), bf(ffn1_down[l]), gf,
                  final_norm=False)
        da_q, da_k, da_v, sw_q, sw_kk, sw_vv = _inproj(xt, row(norm_mix[l]), w_qkv, widths, scales)
        o_da = _da_attention(da_q, da_k, da_v, da_lambda[l], row(da_subnorm[l]),
                             _alibi_slopes(DA_HEADS), B=B, S=S, lam_init=lam_init)
        o_sw = _sw_attention(sw_q, sw_kk, sw_vv, _alibi_slopes(SW_Q_HEADS), swa_sink[l], B=B, S=S)
        xt = _merge(xt, row(norm_mix[l]), bf(wl[:, o_gate:]), row(b_gate[l]), o_da, o_sw,
                    bf(w_proj_da[l]), bf(w_proj_swa[l]), bf(w_out[l]))
        last = l == depth - 1
        xt = _ffn(xt, row(norm_ffn2[l]), bf(ffn2_gate[l]), bf(ffn2_up[l]), bf(ffn2_down[l]), gf,
                  final_norm=last)
    return xt.reshape(B, S, D)
```

```python
import functools
import math

import jax
import jax.numpy as jnp
from jax import lax
from jax.experimental import pallas as pl
from jax.experimental.pallas import tpu as pltpu

F32 = jnp.float32
BF16 = jnp.bfloat16

RMS_EPS = 1e-6
LOG2E = math.log2(math.e)
LANES = 128
HALF = 64
VMEM_LIMIT = 56 * 1024 * 1024
MASKED_DIST = 1e30

DA_HEADS = 8
DA_TQ = 256
DA_SUBBLOCKS = 2
SW_Q_HEADS = 16
SW_KV_HEADS = 4
SW_GROUP = SW_Q_HEADS // SW_KV_HEADS
SW_BLOCKS_PER_STEP = 4
WINDOW = 128
Q_BLOCK = 128
SW_WIN = 3 * Q_BLOCK


def _rmsnorm(x, g):
    return x * lax.rsqrt(jnp.mean(x * x, axis=-1, keepdims=True) + RMS_EPS) * g


def _const_spec(shape):
    return pl.BlockSpec(shape, lambda *_: (0,) * len(shape), pipeline_mode=pl.Buffered(1))


def _params(n_axes):
    return pltpu.CompilerParams(
        dimension_semantics=("parallel",) * n_axes, vmem_limit_bytes=VMEM_LIMIT)


def _ffn_kernel(x_ref, g_ref, wg_ref, wu_ref, wd_ref, gf_ref, o_ref, a_scr, *, ff_chunk, final_norm):
    x = x_ref[...]
    hb = _rmsnorm(x, g_ref[...]).astype(BF16)
    d_ff = wg_ref.shape[1]
    for c in range(d_ff // ff_chunk):
        sl = slice(c * ff_chunk, (c + 1) * ff_chunk)
        gate = jnp.dot(hb, wg_ref[:, sl], preferred_element_type=F32)
        up = jnp.dot(hb, wu_ref[:, sl], preferred_element_type=F32)
        a_scr[:, sl] = (gate * jax.nn.sigmoid(gate) * up).astype(BF16)
    y = jnp.dot(a_scr[...], wd_ref[...], preferred_element_type=F32)
    x = x + 0.5 * y
    if final_norm:
        x = _rmsnorm(x, gf_ref[...])
    o_ref[...] = x


def _ffn(x, g, wg, wu, wd, gf, *, final_norm, tm=512, ff_chunk=256):
    T, D = x.shape
    d_ff = wg.shape[1]
    kern = functools.partial(_ffn_kernel, ff_chunk=ff_chunk, final_norm=final_norm)
    return pl.pallas_call(
        kern,
        out_shape=jax.ShapeDtypeStruct((T, D), F32),
        grid=(T // tm,),
        in_specs=[
            pl.BlockSpec((tm, D), lambda i: (i, 0)),
            _const_spec((1, D)),
            _const_spec((D, d_ff)),
            _const_spec((D, d_ff)),
            _const_spec((d_ff, D)),
            _const_spec((1, D)),
        ],
        out_specs=pl.BlockSpec((tm, D), lambda i: (i, 0)),
        scratch_shapes=[pltpu.VMEM((tm, d_ff), BF16)],
        compiler_params=_params(1),
        name="ffn_final" if final_norm else "ffn",
    )(x, g, wg, wu, wd, gf)


def _inproj_kernel(x_ref, g_ref, w_ref, wkt_ref, kt_ref, *o_refs, widths, scales):
    hb = _rmsnorm(x_ref[...], g_ref[...]).astype(BF16)
    kt = lax.dot_general(wkt_ref[...], hb, (((1,), (1,)), ((), ())), preferred_element_type=F32)
    kt_ref[...] = kt.astype(BF16)
    off = 0
    for o_ref, width, scale in zip(o_refs, widths, scales):
        for c in range(0, width, 512):
            cw = min(512, width - c)
            y = jnp.dot(hb, w_ref[:, off + c:off + c + cw], preferred_element_type=F32)
            if scale != 1.0:
                y = y * scale
            o_ref[:, c:c + cw] = y.astype(BF16)
        off += width


def _inproj(x, g, w, wkt, widths, scales, *, tm=512):
    T, D = x.shape
    kern = functools.partial(_inproj_kernel, widths=widths, scales=scales)
    return pl.pallas_call(
        kern,
        out_shape=[jax.ShapeDtypeStruct((wkt.shape[0], T), BF16)]
        + [jax.ShapeDtypeStruct((T, wd), BF16) for wd in widths],
        grid=(T // tm,),
        in_specs=[
            pl.BlockSpec((tm, D), lambda i: (i, 0)),
            _const_spec((1, D)),
            _const_spec(w.shape),
            _const_spec(wkt.shape),
        ],
        out_specs=[pl.BlockSpec((wkt.shape[0], tm), lambda i: (0, i))]
        + [pl.BlockSpec((tm, wd), lambda i: (i, 0)) for wd in widths],
        compiler_params=_params(1),
        name="in_proj",
    )(x, g, w, wkt)


def _da_kernel(slopes_ref, q_ref, kt_ref, v_ref, dist_ref, lam_ref, sub_ref, o_ref, s_scr, *, lam_init):
    h = pl.program_id(1)
    step = pl.program_id(2)
    tq = DA_TQ
    S = kt_ref.shape[1]
    nk = S // tq
    slope2 = slopes_ref[h] * LOG2E

    lp = lam_ref[...]
    lam = (jnp.exp(jnp.sum(lp[0:1] * lp[1:2], axis=-1, keepdims=True))
           - jnp.exp(jnp.sum(lp[2:3] * lp[3:4], axis=-1, keepdims=True)) + lam_init)
    lane = lax.broadcasted_iota(jnp.int32, (tq, LANES), 1)
    low = lane < HALF

    def fold(x, op):
        acc = x[:, :LANES]
        for t in range(1, tq // LANES):
            acc = op(acc, x[:, t * LANES:(t + 1) * LANES])
        return acc

    def start_block(sub):
        q = q_ref[sub * tq:(sub + 1) * tq, :]
        zero = jnp.zeros_like(q)
        return dict(sub=sub, qi=step * DA_SUBBLOCKS + sub, slot=sub % 2,
                    qm=[jnp.where(low, q, zero), jnp.where(low, zero, q)], mrun=[None, None])

    def score_piece(st, J):
        cols = slice(J * tq, (J + 1) * tq)
        bias = dist_ref[st["qi"] - J + (nk - 1)] * slope2
        kblk = kt_ref[:, cols]
        for c in range(2):
            s = jnp.dot(st["qm"][c], kblk, preferred_element_type=F32) - bias
            s_scr[st["slot"], c, :, cols] = s
            f = fold(s, jnp.maximum)
            st["mrun"][c] = f if J == 0 else jnp.maximum(st["mrun"][c], f)

    def start_values(st):
        st["m"] = [jnp.max(mr, axis=-1, keepdims=True) for mr in st["mrun"]]
        st["lrun"] = [None, None]
        st["o"] = [None, None]

    def value_piece(st, J):
        cols = slice(J * tq, (J + 1) * tq)
        vblk = v_ref[cols, :]
        for c in range(2):
            e = jnp.exp2(s_scr[st["slot"], c, :, cols] - st["m"][c])
            f = fold(e, jnp.add)
            pv = jnp.dot(e.astype(BF16), vblk, preferred_element_type=F32)
            st["lrun"][c] = f if J == 0 else st["lrun"][c] + f
            st["o"][c] = pv if J == 0 else st["o"][c] + pv

    def finish_block(st):
        l1, l2 = (jnp.sum(lr, axis=-1, keepdims=True) for lr in st["lrun"])
        o = st["o"][0] * (1.0 / l1) - st["o"][1] * (lam / l2)
        o = _rmsnorm(o, sub_ref[...]) * (1.0 - lam_init)
        o_ref[st["sub"] * tq:(st["sub"] + 1) * tq, :] = o.astype(BF16)

    prev = None
    for stage in range(DA_SUBBLOCKS + 1):
        cur = start_block(stage) if stage < DA_SUBBLOCKS else None
        if prev is not None:
            start_values(prev)
        for J in range(nk):
            if prev is not None:
                value_piece(prev, J)
            if cur is not None:
                score_piece(cur, J)
        if prev is not None:
            finish_block(prev)
        prev = cur


def _da_attention(q, kt, v, dist, da_lambda, subnorm, slopes, *, B, S, lam_init):
    T = q.shape[0]
    tq = DA_TQ * DA_SUBBLOCKS
    nq = S // tq
    kern = functools.partial(_da_kernel, lam_init=lam_init)
    return pl.pallas_call(
        kern,
        out_shape=jax.ShapeDtypeStruct((T, DA_HEADS * LANES), BF16),
        grid=(B, DA_HEADS, nq),
        in_specs=[
            pl.BlockSpec(memory_space=pltpu.SMEM),
            pl.BlockSpec((tq, LANES), lambda b, h, i: (b * nq + i, h)),
            pl.BlockSpec((LANES, S), lambda b, h, i: (h, b)),
            pl.BlockSpec((S, LANES), lambda b, h, i: (b, h)),
            _const_spec(dist.shape),
            _const_spec((4, HALF)),
            _const_spec((1, LANES)),
        ],
        out_specs=pl.BlockSpec((tq, LANES), lambda b, h, i: (b * nq + i, h)),
        scratch_shapes=[pltpu.VMEM((2, 2, DA_TQ, S), F32)],
        compiler_params=_params(3),
        name="da_attn",
    )(slopes, q, kt, v, dist, da_lambda, subnorm)


def _da_distance_table(S):
    nk = S // DA_TQ
    t = jnp.arange(2 * nk - 1, dtype=jnp.int32)[:, None, None] - (nk - 1)
    r = jnp.arange(DA_TQ, dtype=jnp.int32)[None, :, None]
    m = jnp.arange(DA_TQ, dtype=jnp.int32)[None, None, :]
    return jnp.abs(t * DA_TQ + r - m).astype(F32)


def _sw_kernel(slopes_ref, sink_ref, q_ref, kk_ref, vv_ref, dist_ref, o_ref):
    j = pl.program_id(1)
    ch = pl.program_id(2)
    S = kk_ref.shape[0]
    nb = S // Q_BLOCK
    lane = lax.broadcasted_iota(jnp.int32, (Q_BLOCK, LANES), 1)
    low = lane < HALF
    dn = (((1,), (1,)), ((), ()))
    slope2 = [slopes_ref[j * SW_GROUP + g] * LOG2E for g in range(SW_GROUP)]
    sink2 = [sink_ref[j * SW_GROUP + g] * LOG2E for g in range(SW_GROUP)]

    for r in range(SW_BLOCKS_PER_STEP):
        n = ch * SW_BLOCKS_PER_STEP + r
        start = pl.multiple_of(jnp.clip((n - 1) * Q_BLOCK, 0, S - SW_WIN), Q_BLOCK)
        variant = jnp.where(n == 0, 0, jnp.where(n == nb - 1, 2, 1))
        dist = dist_ref[variant]
        kwin = kk_ref[pl.ds(start, SW_WIN), :]
        vwin = vv_ref[pl.ds(start, SW_WIN), :]
        rows = slice(r * Q_BLOCK, (r + 1) * Q_BLOCK)
        stacked = []
        for g in range(SW_GROUP):
            qpair = q_ref[rows, (g // 2) * LANES:(g // 2 + 1) * LANES]
            stacked.append(jnp.where(low if g % 2 == 0 else ~low, qpair, jnp.zeros_like(qpair)))
        s = lax.dot_general(jnp.concatenate(stacked, axis=0), kwin, dn,
                            preferred_element_type=F32)
        es, inv = [], []
        for g in range(SW_GROUP):
            sg = s[g * Q_BLOCK:(g + 1) * Q_BLOCK] - dist * slope2[g]
            m = jnp.maximum(jnp.max(sg, axis=-1, keepdims=True), sink2[g])
            e = jnp.exp2(sg - m)
            inv.append(1.0 / (jnp.sum(e, axis=-1, keepdims=True) + jnp.exp2(sink2[g] - m)))
            es.append(e.astype(BF16))
        o = jnp.dot(jnp.concatenate(es, axis=0), vwin, preferred_element_type=F32)
        og = [o[g * Q_BLOCK:(g + 1) * Q_BLOCK] * inv[g] for g in range(SW_GROUP)]
        for t in range(SW_GROUP // 2):
            o_ref[rows, t * LANES:(t + 1) * LANES] = jnp.where(low, og[2 * t], og[2 * t + 1]).astype(BF16)


def _sw_attention(q, kk, vv, dist, slopes, sink, *, B, S):
    T = q.shape[0]
    tq = SW_BLOCKS_PER_STEP * Q_BLOCK
    nq = S // tq
    gw = SW_GROUP * HALF
    return pl.pallas_call(
        _sw_kernel,
        out_shape=jax.ShapeDtypeStruct((T, SW_Q_HEADS * HALF), BF16),
        grid=(B, SW_KV_HEADS, nq),
        in_specs=[
            pl.BlockSpec(memory_space=pltpu.SMEM),
            pl.BlockSpec(memory_space=pltpu.SMEM),
            pl.BlockSpec((tq, gw), lambda b, j, n: (b * nq + n, j)),
            pl.BlockSpec((S, LANES), lambda b, j, n: (b, j)),
            pl.BlockSpec((S, LANES), lambda b, j, n: (b, j)),
            _const_spec(dist.shape),
        ],
        out_specs=pl.BlockSpec((tq, gw), lambda b, j, n: (b * nq + n, j)),
        compiler_params=_params(3),
        name="sw_attn",
    )(slopes, sink, q, kk, vv, dist)


def _sw_distance_table():
    shift = jnp.arange(3, dtype=jnp.int32)[:, None, None] * Q_BLOCK
    r = jnp.arange(Q_BLOCK, dtype=jnp.int32)[None, :, None]
    c = jnp.arange(SW_WIN, dtype=jnp.int32)[None, None, :]
    d = jnp.abs(shift + r - c)
    return jnp.where(d <= WINDOW, d.astype(F32), MASKED_DIST)


def _merge_kernel(x_ref, g_ref, wgate_ref, bg_ref, oda_ref, osw_ref, pa_ref, pb_ref, wo_ref, o_ref):
    x = x_ref[...]
    D = x.shape[1]
    hb = _rmsnorm(x, g_ref[...]).astype(BF16)
    gates = jnp.dot(hb, wgate_ref[...], preferred_element_type=F32) + bg_ref[...]
    gates = jax.nn.sigmoid(gates)
    ya = jnp.dot(oda_ref[...], pa_ref[...], preferred_element_type=F32)
    yb = jnp.dot(osw_ref[...], pb_ref[...], preferred_element_type=F32)
    merged = gates[:, :D] * ya + gates[:, D:] * yb
    o_ref[...] = x + jnp.dot(merged.astype(BF16), wo_ref[...], preferred_element_type=F32)


def _merge(x, g, wgate, bg, oda, osw, pa, pb, wo, *, tm=512):
    T, D = x.shape
    row = lambda i: (i, 0)
    return pl.pallas_call(
        _merge_kernel,
        out_shape=jax.ShapeDtypeStruct((T, D), F32),
        grid=(T // tm,),
        in_specs=[
            pl.BlockSpec((tm, D), row),
            _const_spec((1, D)),
            _const_spec((D, 2 * D)),
            _const_spec((1, 2 * D)),
            pl.BlockSpec((tm, D), row),
            pl.BlockSpec((tm, D), row),
            _const_spec((D, D)),
            _const_spec((D, D)),
            _const_spec((D, D)),
        ],
        out_specs=pl.BlockSpec((tm, D), row),
        compiler_params=_params(1),
        name="merge",
    )(x, g, wgate, bg, oda, osw, pa, pb, wo)


def _alibi_slopes(n):
    return 2.0 ** (-8.0 * jnp.arange(1, n + 1, dtype=F32) / n)


def kernel(x, norm_ffn1, ffn1_gate, ffn1_up, ffn1_down, norm_mix, w_in, b_gate, da_lambda, da_subnorm, swa_sink, w_proj_da, w_proj_swa, w_out, norm_ffn2, ffn2_gate, ffn2_up, ffn2_down, norm_final):
    B, S, D = x.shape
    depth = norm_ffn1.shape[0]
    assert depth >= 1, "the final RMSNorm is fused into the last layer's second FFN"
    T = B * S
    xt = x.reshape(T, D)
    bf = lambda a: a.astype(BF16)
    row = lambda a: a.reshape(1, -1)
    gf = row(norm_final)

    da_w = DA_HEADS * LANES
    swq_w = SW_Q_HEADS * HALF
    swkv_w = SW_KV_HEADS * HALF
    o_k, o_v = da_w, 2 * da_w
    o_sq = o_v + da_w
    o_sk = o_sq + swq_w
    o_sv = o_sk + swkv_w
    o_gate = o_sv + swkv_w

    def dup_heads(w):
        w = w.reshape(D, SW_KV_HEADS, 1, HALF)
        return jnp.broadcast_to(w, (D, SW_KV_HEADS, 2, HALF)).reshape(D, SW_KV_HEADS * LANES)

    widths = (da_w, da_w, swq_w, SW_KV_HEADS * LANES, SW_KV_HEADS * LANES)
    qscale = HALF ** -0.5 * LOG2E
    scales = (qscale, 1.0, qscale, 1.0, 1.0)
    da_dist = _da_distance_table(S)
    sw_dist = _sw_distance_table()

    for l in range(depth):
        lam_init = 0.8 - 0.6 * math.exp(-0.3 * l)
        wl = w_in[l]
        w_cols = bf(jnp.concatenate(
            [wl[:, :o_k], wl[:, o_v:o_sk], dup_heads(wl[:, o_sk:o_sv]), dup_heads(wl[:, o_sv:o_gate])],
            axis=1))
        w_kt = bf(wl[:, o_k:o_v].T)

        xt = _ffn(xt, row(norm_ffn1[l]), bf(ffn1_gate[l]), bf(ffn1_up[l]), bf(ffn1_down[l]), gf,
                  final_norm=False)
        da_kt, da_q, da_v, sw_q, sw_kk, sw_vv = _inproj(xt, row(norm_mix[l]), w_cols, w_kt, widths, scales)
        o_da = _da_attention(da_q, da_kt, da_v, da_dist, da_lambda[l], row(da_subnorm[l]),
                             _alibi_slopes(DA_HEADS), B=B, S=S, lam_init=lam_init)
        o_sw = _sw_attention(sw_q, sw_kk, sw_vv, sw_dist, _alibi_slopes(SW_Q_HEADS), swa_sink[l], B=B, S=S)
        xt = _merge(xt, row(norm_mix[l]), bf(wl[:, o_gate:]), row(b_gate[l]), o_da, o_sw,
                    bf(w_proj_da[l]), bf(w_proj_swa[l]), bf(w_out[l]))
        xt = _ffn(xt, row(norm_ffn2[l]), bf(ffn2_gate[l]), bf(ffn2_up[l]), bf(ffn2_down[l]), gf,
                  final_norm=(l == depth - 1))
    return xt.reshape(B, S, D)
```

```python
import functools
import math

import jax
import jax.numpy as jnp
from jax import lax
from jax.experimental import pallas as pl
from jax.experimental.pallas import tpu as pltpu

F32 = jnp.float32
BF16 = jnp.bfloat16

RMS_EPS = 1e-6
LOG2E = math.log2(math.e)
LANES = 128
HALF = 64
VMEM_LIMIT = 56 * 1024 * 1024
MASKED_DIST = 1e30

DA_HEADS = 8
DA_TQ = 256
DA_SUBBLOCKS = 4
DA_VAUG_ROWS = LANES + 16
DA_MAX_ROWS = 32
SW_Q_HEADS = 16
SW_KV_HEADS = 4
SW_GROUP = SW_Q_HEADS // SW_KV_HEADS
SW_BLOCKS_PER_STEP = 4
WINDOW = 128
Q_BLOCK = 128
SW_WIN = 3 * Q_BLOCK


def _rmsnorm(x, g):
    return x * lax.rsqrt(jnp.mean(x * x, axis=-1, keepdims=True) + RMS_EPS) * g


def _const_spec(shape):
    return pl.BlockSpec(shape, lambda *_: (0,) * len(shape), pipeline_mode=pl.Buffered(1))


def _params(n_axes):
    return pltpu.CompilerParams(
        dimension_semantics=("parallel",) * n_axes, vmem_limit_bytes=VMEM_LIMIT)


def _ffn_kernel(x_ref, g_ref, wg_ref, wu_ref, wd_ref, gf_ref, o_ref, a_scr, *, ff_chunk, final_norm):
    x = x_ref[...]
    hb = _rmsnorm(x, g_ref[...]).astype(BF16)
    d_ff = wg_ref.shape[1]
    for c in range(d_ff // ff_chunk):
        sl = slice(c * ff_chunk, (c + 1) * ff_chunk)
        gate = jnp.dot(hb, wg_ref[:, sl], preferred_element_type=F32)
        up = jnp.dot(hb, wu_ref[:, sl], preferred_element_type=F32)
        a_scr[:, sl] = (gate * jax.nn.sigmoid(gate) * up).astype(BF16)
    y = jnp.dot(a_scr[...], wd_ref[...], preferred_element_type=F32)
    x = x + 0.5 * y
    if final_norm:
        x = _rmsnorm(x, gf_ref[...])
    o_ref[...] = x


def _ffn(x, g, wg, wu, wd, gf, *, final_norm, tm=512, ff_chunk=256):
    T, D = x.shape
    d_ff = wg.shape[1]
    kern = functools.partial(_ffn_kernel, ff_chunk=ff_chunk, final_norm=final_norm)
    return pl.pallas_call(
        kern,
        out_shape=jax.ShapeDtypeStruct((T, D), F32),
        grid=(T // tm,),
        in_specs=[
            pl.BlockSpec((tm, D), lambda i: (i, 0)),
            _const_spec((1, D)),
            _const_spec((D, d_ff)),
            _const_spec((D, d_ff)),
            _const_spec((d_ff, D)),
            _const_spec((1, D)),
        ],
        out_specs=pl.BlockSpec((tm, D), lambda i: (i, 0)),
        scratch_shapes=[pltpu.VMEM((tm, d_ff), BF16)],
        compiler_params=_params(1),
        name="ffn_final" if final_norm else "ffn",
    )(x, g, wg, wu, wd, gf)


def _inproj_kernel(x_ref, g_ref, w_ref, wt_ref, qt_ref, vt_ref, *o_refs, widths, scales, qscale):
    hb = _rmsnorm(x_ref[...], g_ref[...]).astype(BF16)
    nt = (((1,), (1,)), ((), ()))
    da_w = qt_ref.shape[0]
    qt = lax.dot_general(wt_ref[:da_w, :], hb, nt, preferred_element_type=F32)
    qt_ref[...] = (qt * qscale).astype(BF16)
    vt = lax.dot_general(wt_ref[da_w:, :], hb, nt, preferred_element_type=F32)
    vt_ref[...] = vt.astype(BF16)
    off = 0
    for o_ref, width, scale in zip(o_refs, widths, scales):
        for c in range(0, width, 512):
            cw = min(512, width - c)
            y = jnp.dot(hb, w_ref[:, off + c:off + c + cw], preferred_element_type=F32)
            if scale != 1.0:
                y = y * scale
            o_ref[:, c:c + cw] = y.astype(BF16)
        off += width


def _inproj(x, g, w, wt, widths, scales, qscale, *, tm=512):
    T, D = x.shape
    da_w = wt.shape[0] // 2
    kern = functools.partial(_inproj_kernel, widths=widths, scales=scales, qscale=qscale)
    return pl.pallas_call(
        kern,
        out_shape=[jax.ShapeDtypeStruct((da_w, T), BF16)] * 2
        + [jax.ShapeDtypeStruct((T, wd), BF16) for wd in widths],
        grid=(T // tm,),
        in_specs=[
            pl.BlockSpec((tm, D), lambda i: (i, 0)),
            _const_spec((1, D)),
            _const_spec(w.shape),
            _const_spec(wt.shape),
        ],
        out_specs=[pl.BlockSpec((da_w, tm), lambda i: (0, i))] * 2
        + [pl.BlockSpec((tm, wd), lambda i: (i, 0)) for wd in widths],
        compiler_params=_params(1),
        name="in_proj",
    )(x, g, w, wt)


def _da_kernel(slopes_ref, qt_ref, k_ref, vt_ref, dist_ref, lam_ref, sub_ref, o_ref,
               s_scr, vaug_scr, *, lam_init):
    h = pl.program_id(1)
    step = pl.program_id(2)
    tq = DA_TQ
    S = k_ref.shape[0]
    nk = S // tq
    slope2 = slopes_ref[h] * LOG2E

    lp = lam_ref[...]
    lam = (jnp.exp(jnp.sum(lp[0:1] * lp[1:2], axis=-1, keepdims=True))
           - jnp.exp(jnp.sum(lp[2:3] * lp[3:4], axis=-1, keepdims=True)) + lam_init)
    top = lax.broadcasted_iota(jnp.int32, (LANES, tq), 0) < HALF

    @pl.when(step == 0)
    def _():
        vaug_scr[:LANES, :] = vt_ref[...]
        first = lax.broadcasted_iota(jnp.int32, (DA_VAUG_ROWS - LANES, S), 0) == 0
        vaug_scr[LANES:, :] = jnp.where(first, 1.0, 0.0).astype(BF16)

    def start_scores(sub):
        qt = qt_ref[:, sub * tq:(sub + 1) * tq]
        zero = jnp.zeros_like(qt)
        return dict(sub=sub, qi=step * DA_SUBBLOCKS + sub, slot=sub % 2,
                    qm=[jnp.where(top, qt, zero), jnp.where(top, zero, qt)], mrun=[None, None])

    def score_piece(st, J):
        keys = slice(J * tq, (J + 1) * tq)
        bias = dist_ref[J - st["qi"] + (nk - 1)] * slope2
        kblk = k_ref[keys, :]
        for c in range(2):
            s = jnp.dot(kblk, st["qm"][c], preferred_element_type=F32) - bias
            s_scr[st["slot"], c, keys, :] = s
            f = jnp.max(s.reshape(tq // DA_MAX_ROWS, DA_MAX_ROWS, tq), axis=0)
            st["mrun"][c] = f if J == 0 else jnp.maximum(st["mrun"][c], f)

    def start_values(st):
        st["m"] = [jnp.max(mr, axis=0, keepdims=True) for mr in st["mrun"]]
        st["acc"] = [None, None]

    def value_piece(st, J):
        keys = slice(J * tq, (J + 1) * tq)
        for c in range(2):
            e = jnp.exp2((s_scr[st["slot"], c, keys, :] - st["m"][c]).astype(BF16))
            pv = jnp.dot(vaug_scr[:, keys], e, preferred_element_type=F32)
            st["acc"][c] = pv if J == 0 else st["acc"][c] + pv

    def finish(st):
        (o1, l1), (o2, l2) = ((a[:LANES], a[LANES:LANES + 1]) for a in st["acc"])
        ot = o1 * (1.0 / l1) - o2 * (lam / l2)
        o = _rmsnorm(ot.T, sub_ref[...]) * (1.0 - lam_init)
        o_ref[st["sub"] * tq:(st["sub"] + 1) * tq, :] = o.astype(BF16)

    valuing = None
    for stage in range(DA_SUBBLOCKS + 1):
        scoring = start_scores(stage) if stage < DA_SUBBLOCKS else None
        if valuing is not None:
            start_values(valuing)
        for J in range(nk):
            if valuing is not None:
                value_piece(valuing, J)
            if scoring is not None:
                score_piece(scoring, J)
        if valuing is not None:
            finish(valuing)
        valuing = scoring


def _da_attention(qt, k, vt, dist, da_lambda, subnorm, slopes, *, B, S, lam_init):
    T = k.shape[0]
    tq = DA_TQ * DA_SUBBLOCKS
    nq = S // tq
    kern = functools.partial(_da_kernel, lam_init=lam_init)
    return pl.pallas_call(
        kern,
        out_shape=jax.ShapeDtypeStruct((T, DA_HEADS * LANES), BF16),
        grid=(B, DA_HEADS, nq),
        in_specs=[
            pl.BlockSpec(memory_space=pltpu.SMEM),
            pl.BlockSpec((LANES, tq), lambda b, h, i: (h, b * nq + i)),
            pl.BlockSpec((S, LANES), lambda b, h, i: (b, h)),
            pl.BlockSpec((LANES, S), lambda b, h, i: (h, b)),
            _const_spec(dist.shape),
            _const_spec((4, HALF)),
            _const_spec((1, LANES)),
        ],
        out_specs=pl.BlockSpec((tq, LANES), lambda b, h, i: (b * nq + i, h)),
        scratch_shapes=[
            pltpu.VMEM((2, 2, S, DA_TQ), F32),
            pltpu.VMEM((DA_VAUG_ROWS, S), BF16),
        ],
        compiler_params=pltpu.CompilerParams(
            dimension_semantics=("parallel", "parallel", "arbitrary"), vmem_limit_bytes=VMEM_LIMIT),
        name="da_attn",
    )(slopes, qt, k, vt, dist, da_lambda, subnorm)


def _da_distance_table(S):
    nk = S // DA_TQ
    t = jnp.arange(2 * nk - 1, dtype=jnp.int32)[:, None, None] - (nk - 1)
    r = jnp.arange(DA_TQ, dtype=jnp.int32)[None, :, None]
    m = jnp.arange(DA_TQ, dtype=jnp.int32)[None, None, :]
    return jnp.abs(t * DA_TQ + r - m).astype(F32)


def _sw_kernel(slopes_ref, sink_ref, q_ref, kk_ref, vv_ref, dist_ref, o_ref):
    j = pl.program_id(1)
    ch = pl.program_id(2)
    S = kk_ref.shape[0]
    nb = S // Q_BLOCK
    lane = lax.broadcasted_iota(jnp.int32, (Q_BLOCK, LANES), 1)
    low = lane < HALF
    dn = (((1,), (1,)), ((), ()))
    slope2 = [slopes_ref[j * SW_GROUP + g] * LOG2E for g in range(SW_GROUP)]
    sink2 = [sink_ref[j * SW_GROUP + g] * LOG2E for g in range(SW_GROUP)]

    for r in range(SW_BLOCKS_PER_STEP):
        n = ch * SW_BLOCKS_PER_STEP + r
        start = pl.multiple_of(jnp.clip((n - 1) * Q_BLOCK, 0, S - SW_WIN), Q_BLOCK)
        variant = jnp.where(n == 0, 0, jnp.where(n == nb - 1, 2, 1))
        dist = dist_ref[variant]
        kwin = kk_ref[pl.ds(start, SW_WIN), :]
        vwin = vv_ref[pl.ds(start, SW_WIN), :]
        rows = slice(r * Q_BLOCK, (r + 1) * Q_BLOCK)
        stacked = []
        for g in range(SW_GROUP):
            qpair = q_ref[rows, (g // 2) * LANES:(g // 2 + 1) * LANES]
            stacked.append(jnp.where(low if g % 2 == 0 else ~low, qpair, jnp.zeros_like(qpair)))
        s = lax.dot_general(jnp.concatenate(stacked, axis=0), kwin, dn,
                            preferred_element_type=F32)
        es, inv = [], []
        for g in range(SW_GROUP):
            sg = s[g * Q_BLOCK:(g + 1) * Q_BLOCK] - dist * slope2[g]
            m = jnp.maximum(jnp.max(sg, axis=-1, keepdims=True), sink2[g])
            e = jnp.exp2(sg - m)
            inv.append(1.0 / (jnp.sum(e, axis=-1, keepdims=True) + jnp.exp2(sink2[g] - m)))
            es.append(e.astype(BF16))
        o = jnp.dot(jnp.concatenate(es, axis=0), vwin, preferred_element_type=F32)
        og = [o[g * Q_BLOCK:(g + 1) * Q_BLOCK] * inv[g] for g in range(SW_GROUP)]
        for t in range(SW_GROUP // 2):
            o_ref[rows, t * LANES:(t + 1) * LANES] = jnp.where(low, og[2 * t], og[2 * t + 1]).astype(BF16)


def _sw_attention(q, kk, vv, dist, slopes, sink, *, B, S):
    T = q.shape[0]
    tq = SW_BLOCKS_PER_STEP * Q_BLOCK
    nq = S // tq
    gw = SW_GROUP * HALF
    return pl.pallas_call(
        _sw_kernel,
        out_shape=jax.ShapeDtypeStruct((T, SW_Q_HEADS * HALF), BF16),
        grid=(B, SW_KV_HEADS, nq),
        in_specs=[
            pl.BlockSpec(memory_space=pltpu.SMEM),
            pl.BlockSpec(memory_space=pltpu.SMEM),
            pl.BlockSpec((tq, gw), lambda b, j, n: (b * nq + n, j)),
            pl.BlockSpec((S, LANES), lambda b, j, n: (b, j)),
            pl.BlockSpec((S, LANES), lambda b, j, n: (b, j)),
            _const_spec(dist.shape),
        ],
        out_specs=pl.BlockSpec((tq, gw), lambda b, j, n: (b * nq + n, j)),
        compiler_params=_params(3),
        name="sw_attn",
    )(slopes, sink, q, kk, vv, dist)


def _sw_distance_table():
    shift = jnp.arange(3, dtype=jnp.int32)[:, None, None] * Q_BLOCK
    r = jnp.arange(Q_BLOCK, dtype=jnp.int32)[None, :, None]
    c = jnp.arange(SW_WIN, dtype=jnp.int32)[None, None, :]
    d = jnp.abs(shift + r - c)
    return jnp.where(d <= WINDOW, d.astype(F32), MASKED_DIST)


def _merge_kernel(x_ref, g_ref, wgate_ref, bg_ref, oda_ref, osw_ref, pa_ref, pb_ref, wo_ref, o_ref):
    x = x_ref[...]
    D = x.shape[1]
    hb = _rmsnorm(x, g_ref[...]).astype(BF16)
    gates = jnp.dot(hb, wgate_ref[...], preferred_element_type=F32) + bg_ref[...]
    gates = jax.nn.sigmoid(gates)
    ya = jnp.dot(oda_ref[...], pa_ref[...], preferred_element_type=F32)
    yb = jnp.dot(osw_ref[...], pb_ref[...], preferred_element_type=F32)
    merged = gates[:, :D] * ya + gates[:, D:] * yb
    o_ref[...] = x + jnp.dot(merged.astype(BF16), wo_ref[...], preferred_element_type=F32)


def _merge(x, g, wgate, bg, oda, osw, pa, pb, wo, *, tm=512):
    T, D = x.shape
    row = lambda i: (i, 0)
    return pl.pallas_call(
        _merge_kernel,
        out_shape=jax.ShapeDtypeStruct((T, D), F32),
        grid=(T // tm,),
        in_specs=[
            pl.BlockSpec((tm, D), row),
            _const_spec((1, D)),
            _const_spec((D, 2 * D)),
            _const_spec((1, 2 * D)),
            pl.BlockSpec((tm, D), row),
            pl.BlockSpec((tm, D), row),
            _const_spec((D, D)),
            _const_spec((D, D)),
            _const_spec((D, D)),
        ],
        out_specs=pl.BlockSpec((tm, D), row),
        compiler_params=_params(1),
        name="merge",
    )(x, g, wgate, bg, oda, osw, pa, pb, wo)


def _alibi_slopes(n):
    return 2.0 ** (-8.0 * jnp.arange(1, n + 1, dtype=F32) / n)


def kernel(x, norm_ffn1, ffn1_gate, ffn1_up, ffn1_down, norm_mix, w_in, b_gate, da_lambda, da_subnorm, swa_sink, w_proj_da, w_proj_swa, w_out, norm_ffn2, ffn2_gate, ffn2_up, ffn2_down, norm_final):
    B, S, D = x.shape
    depth = norm_ffn1.shape[0]
    assert depth >= 1, "the final RMSNorm is fused into the last layer's second FFN"
    T = B * S
    xt = x.reshape(T, D)
    bf = lambda a: a.astype(BF16)
    row = lambda a: a.reshape(1, -1)
    gf = row(norm_final)

    da_w = DA_HEADS * LANES
    swq_w = SW_Q_HEADS * HALF
    swkv_w = SW_KV_HEADS * HALF
    o_k, o_v = da_w, 2 * da_w
    o_sq = o_v + da_w
    o_sk = o_sq + swq_w
    o_sv = o_sk + swkv_w
    o_gate = o_sv + swkv_w

    def dup_heads(w):
        w = w.reshape(D, SW_KV_HEADS, 1, HALF)
        return jnp.broadcast_to(w, (D, SW_KV_HEADS, 2, HALF)).reshape(D, SW_KV_HEADS * LANES)

    widths = (da_w, swq_w, SW_KV_HEADS * LANES, SW_KV_HEADS * LANES)
    qscale = HALF ** -0.5 * LOG2E
    scales = (1.0, qscale, 1.0, 1.0)
    da_dist = _da_distance_table(S)
    sw_dist = _sw_distance_table()

    for l in range(depth):
        lam_init = 0.8 - 0.6 * math.exp(-0.3 * l)
        wl = w_in[l]
        w_cols = bf(jnp.concatenate(
            [wl[:, o_k:o_v], wl[:, o_sq:o_sk], dup_heads(wl[:, o_sk:o_sv]), dup_heads(wl[:, o_sv:o_gate])],
            axis=1))
        w_t = bf(jnp.concatenate([wl[:, :o_k], wl[:, o_v:o_sq]], axis=1).T)

        xt = _ffn(xt, row(norm_ffn1[l]), bf(ffn1_gate[l]), bf(ffn1_up[l]), bf(ffn1_down[l]), gf,
                  final_norm=False)
        da_qt, da_vt, da_k, sw_q, sw_kk, sw_vv = _inproj(
            xt, row(norm_mix[l]), w_cols, w_t, widths, scales, qscale)
        o_da = _da_attention(da_qt, da_k, da_vt, da_dist, da_lambda[l], row(da_subnorm[l]),
                             _alibi_slopes(DA_HEADS), B=B, S=S, lam_init=lam_init)
        o_sw = _sw_attention(sw_q, sw_kk, sw_vv, sw_dist, _alibi_slopes(SW_Q_HEADS), swa_sink[l], B=B, S=S)
        xt = _merge(xt, row(norm_mix[l]), bf(wl[:, o_gate:]), row(b_gate[l]), o_da, o_sw,
                    bf(w_proj_da[l]), bf(w_proj_swa[l]), bf(w_out[l]))
        xt = _ffn(xt, row(norm_ffn2[l]), bf(ffn2_gate[l]), bf(ffn2_up[l]), bf(ffn2_down[l]), gf,
                  final_norm=(l == depth - 1))
    return xt.reshape(B, S, D)
```

```python
import functools
import math

import jax
import jax.numpy as jnp
from jax import lax
from jax.experimental import pallas as pl
from jax.experimental.pallas import tpu as pltpu

F32 = jnp.float32
BF16 = jnp.bfloat16

RMS_EPS = 1e-6
LOG2E = math.log2(math.e)
LANES = 128
HALF = 64
VMEM_LIMIT = 56 * 1024 * 1024
MASKED_DIST = 1e30

DA_HEADS = 8
DA_TQ = 256
DA_SUBBLOCKS = 8
DA_VAUG_ROWS = LANES + 16
DA_SKEW = 0
DA_MAX_ROWS = 32
SW_Q_HEADS = 16
SW_KV_HEADS = 4
SW_GROUP = SW_Q_HEADS // SW_KV_HEADS
SW_BLOCKS_PER_STEP = 4
WINDOW = 128
Q_BLOCK = 128
SW_WIN = 3 * Q_BLOCK


def _rmsnorm(x, g):
    return x * lax.rsqrt(jnp.mean(x * x, axis=-1, keepdims=True) + RMS_EPS) * g


def _const_spec(shape):
    return pl.BlockSpec(shape, lambda *_: (0,) * len(shape), pipeline_mode=pl.Buffered(1))


def _params(n_axes):
    return pltpu.CompilerParams(
        dimension_semantics=("parallel",) * n_axes, vmem_limit_bytes=VMEM_LIMIT)


def _ffn_kernel(x_ref, g_ref, wg_ref, wu_ref, wd_ref, gf_ref, o_ref, a_scr, *, ff_chunk, final_norm):
    x = x_ref[...]
    hb = _rmsnorm(x, g_ref[...]).astype(BF16)
    d_ff = wg_ref.shape[1]
    for c in range(d_ff // ff_chunk):
        sl = slice(c * ff_chunk, (c + 1) * ff_chunk)
        gate = jnp.dot(hb, wg_ref[:, sl], preferred_element_type=F32)
        up = jnp.dot(hb, wu_ref[:, sl], preferred_element_type=F32)
        a_scr[:, sl] = (gate * jax.nn.sigmoid(gate) * up).astype(BF16)
    y = jnp.dot(a_scr[...], wd_ref[...], preferred_element_type=F32)
    x = x + 0.5 * y
    if final_norm:
        x = _rmsnorm(x, gf_ref[...])
    o_ref[...] = x


def _ffn(x, g, wg, wu, wd, gf, *, final_norm, tm=512, ff_chunk=256):
    T, D = x.shape
    d_ff = wg.shape[1]
    kern = functools.partial(_ffn_kernel, ff_chunk=ff_chunk, final_norm=final_norm)
    return pl.pallas_call(
        kern,
        out_shape=jax.ShapeDtypeStruct((T, D), F32),
        grid=(T // tm,),
        in_specs=[
            pl.BlockSpec((tm, D), lambda i: (i, 0)),
            _const_spec((1, D)),
            _const_spec((D, d_ff)),
            _const_spec((D, d_ff)),
            _const_spec((d_ff, D)),
            _const_spec((1, D)),
        ],
        out_specs=pl.BlockSpec((tm, D), lambda i: (i, 0)),
        scratch_shapes=[pltpu.VMEM((tm, d_ff), BF16)],
        compiler_params=_params(1),
        name="ffn_final" if final_norm else "ffn",
    )(x, g, wg, wu, wd, gf)


def _inproj_kernel(x_ref, g_ref, w_ref, wt_ref, qt_ref, vt_ref, *o_refs, widths, scales, qscale):
    hb = _rmsnorm(x_ref[...], g_ref[...]).astype(BF16)
    nt = (((1,), (1,)), ((), ()))
    da_w = qt_ref.shape[0]
    qt = lax.dot_general(wt_ref[:da_w, :], hb, nt, preferred_element_type=F32)
    qt_ref[...] = (qt * qscale).astype(BF16)
    vt = lax.dot_general(wt_ref[da_w:, :], hb, nt, preferred_element_type=F32)
    vt_ref[...] = vt.astype(BF16)
    off = 0
    for o_ref, width, scale in zip(o_refs, widths, scales):
        for c in range(0, width, 512):
            cw = min(512, width - c)
            y = jnp.dot(hb, w_ref[:, off + c:off + c + cw], preferred_element_type=F32)
            if scale != 1.0:
                y = y * scale
            o_ref[:, c:c + cw] = y.astype(BF16)
        off += width


def _inproj(x, g, w, wt, widths, scales, qscale, *, tm=512):
    T, D = x.shape
    da_w = wt.shape[0] // 2
    kern = functools.partial(_inproj_kernel, widths=widths, scales=scales, qscale=qscale)
    return pl.pallas_call(
        kern,
        out_shape=[jax.ShapeDtypeStruct((da_w, T), BF16)] * 2
        + [jax.ShapeDtypeStruct((T, wd), BF16) for wd in widths],
        grid=(T // tm,),
        in_specs=[
            pl.BlockSpec((tm, D), lambda i: (i, 0)),
            _const_spec((1, D)),
            _const_spec(w.shape),
            _const_spec(wt.shape),
        ],
        out_specs=[pl.BlockSpec((da_w, tm), lambda i: (0, i))] * 2
        + [pl.BlockSpec((tm, wd), lambda i: (i, 0)) for wd in widths],
        compiler_params=_params(1),
        name="in_proj",
    )(x, g, w, wt)


def _da_kernel(slopes_ref, qt_ref, qaug_ref, k_ref, kaug_ref, vt_ref, relu_ref, lam_ref, sub_ref, o_ref,
               s_scr, kaug_scr, vaug_scr, *, lam_init):
    h = pl.program_id(1)
    step = pl.program_id(2)
    tq = DA_TQ
    S = k_ref.shape[0]
    nk = S // tq
    slope2 = slopes_ref[h] * LOG2E

    lp = lam_ref[...]
    lam = (jnp.exp(jnp.sum(lp[0:1] * lp[1:2], axis=-1, keepdims=True))
           - jnp.exp(jnp.sum(lp[2:3] * lp[3:4], axis=-1, keepdims=True)) + lam_init)

    @pl.when(step == 0)
    def _():
        k = k_ref[...]
        low = lax.broadcasted_iota(jnp.int32, (S, LANES), 1) < HALF
        for sgn in range(2):
            tab = kaug_ref[sgn]
            kaug_scr[sgn, 0] = jnp.where(low, k, tab)
            kaug_scr[sgn, 1] = jnp.where(low, tab, k)
        first = lax.broadcasted_iota(jnp.int32, (DA_VAUG_ROWS - LANES, tq), 0) == 0
        for J in range(nk):
            vaug_scr[J, :LANES, :] = vt_ref[:, J * tq:(J + 1) * tq]
            vaug_scr[J, LANES:, :] = jnp.where(first, 1.0, 0.0).astype(BF16)

    def key_block(st, d):
        J = st["qi"] + d
        wrapped = J >= nk
        return jnp.where(wrapped, J - nk, J), jnp.where(wrapped, 0, 1)

    def start_scores(sub):
        cols = slice(sub * tq, (sub + 1) * tq)
        qt = qt_ref[:, cols]
        aug = qaug_ref[:, cols]
        return dict(sub=sub, qi=step * DA_SUBBLOCKS + sub, slot=sub % 2, mrun=[None, None],
                    qm=[jnp.concatenate([qt[:HALF], aug], axis=0), jnp.concatenate([aug, qt[HALF:]], axis=0)])

    def score_piece(st, d):
        J, sgn = key_block(st, d)
        if d == 0:
            sgn = 0
            fix = relu_ref[...] * (-2.0 * slope2)
        start = pl.multiple_of(J * tq, tq)
        for c in range(2):
            s = jnp.dot(kaug_scr[sgn, c, pl.ds(start, tq), :], st["qm"][c], preferred_element_type=F32)
            if d == 0:
                s = s + fix
            s_scr[st["slot"], c, d * tq:(d + 1) * tq, :] = s
            f = jnp.max(s.reshape(tq // DA_MAX_ROWS, DA_MAX_ROWS, tq), axis=0)
            st["mrun"][c] = f if d == 0 else jnp.maximum(st["mrun"][c], f)

    def start_values(st):
        st["m"] = [jnp.max(mr, axis=0, keepdims=True) for mr in st["mrun"]]
        st["acc"] = [None, None]

    def value_piece(st, d):
        J, _ = key_block(st, d)
        vblk = vaug_scr[J]
        for c in range(2):
            e = jnp.exp2((s_scr[st["slot"], c, d * tq:(d + 1) * tq, :] - st["m"][c]).astype(BF16))
            pv = jnp.dot(vblk, e, preferred_element_type=F32)
            st["acc"][c] = pv if d == 0 else st["acc"][c] + pv

    def finish(st):
        (o1, l1), (o2, l2) = ((a[:LANES], a[LANES:LANES + 1]) for a in st["acc"])
        ot = o1 * (1.0 / l1) - o2 * (lam / l2)
        o = _rmsnorm(ot.T, sub_ref[...]) * (1.0 - lam_init)
        o_ref[st["sub"] * tq:(st["sub"] + 1) * tq, :] = o.astype(BF16)

    valuing = None
    for stage in range(DA_SUBBLOCKS + 1):
        scoring = start_scores(stage) if stage < DA_SUBBLOCKS else None
        if valuing is not None:
            start_values(valuing)
        for t in range(nk + DA_SKEW):
            if scoring is not None and t < nk:
                score_piece(scoring, t)
            if valuing is not None and t >= DA_SKEW:
                value_piece(valuing, t - DA_SKEW)
        if valuing is not None:
            finish(valuing)
        valuing = scoring


def _da_attention(qt, k, vt, tables, da_lambda, subnorm, slopes, *, B, S, lam_init):
    T = k.shape[0]
    tq = DA_TQ * DA_SUBBLOCKS
    nq = S // tq
    nk = S // DA_TQ
    qaug, kaug, relu = tables
    kern = functools.partial(_da_kernel, lam_init=lam_init)
    return pl.pallas_call(
        kern,
        out_shape=jax.ShapeDtypeStruct((T, DA_HEADS * LANES), BF16),
        grid=(B, DA_HEADS, nq),
        in_specs=[
            pl.BlockSpec(memory_space=pltpu.SMEM),
            pl.BlockSpec((LANES, tq), lambda b, h, i: (h, b * nq + i)),
            pl.BlockSpec((None, HALF, tq), lambda b, h, i: (h, 0, i)),
            pl.BlockSpec((S, LANES), lambda b, h, i: (b, h)),
            pl.BlockSpec((None, 2, S, LANES), lambda b, h, i: (h, 0, 0, 0)),
            pl.BlockSpec((LANES, S), lambda b, h, i: (h, b)),
            _const_spec(relu.shape),
            _const_spec((4, HALF)),
            _const_spec((1, LANES)),
        ],
        out_specs=pl.BlockSpec((tq, LANES), lambda b, h, i: (b * nq + i, h)),
        scratch_shapes=[
            pltpu.VMEM((2, 2, S, DA_TQ), F32),
            pltpu.VMEM((2, 2, S, LANES), BF16),
            pltpu.VMEM((nk, DA_VAUG_ROWS, DA_TQ), BF16),
        ],
        compiler_params=pltpu.CompilerParams(
            dimension_semantics=("parallel", "parallel", "arbitrary"), vmem_limit_bytes=VMEM_LIMIT),
        name="da_attn",
    )(slopes, qt, qaug, k, kaug, vt, relu, da_lambda, subnorm)


def _split3(x):
    def top_bits(y):
        return lax.bitcast_convert_type(lax.bitcast_convert_type(y, jnp.uint32) & jnp.uint32(0xFFFF0000), F32)
    hi = top_bits(x)
    mid = top_bits(x - hi)
    lo = x - hi - mid
    return [p.astype(BF16) for p in (hi, mid, lo)]


def _da_alibi_tables(S, slopes):
    H = slopes.shape[0]
    pos = jnp.arange(S, dtype=F32)
    pieces = _split3((slopes * LOG2E)[:, None] * pos[None, :])
    one = jnp.ones((H, S), BF16)
    zero = jnp.zeros((H, S), BF16)
    key_half = jnp.stack(pieces + [one] * 3 + [zero] * (HALF - 6), axis=-1)
    key_tab = jnp.concatenate([key_half, key_half], axis=-1)
    key_tab = jnp.stack([key_tab, -key_tab], axis=1)
    query_tab = jnp.stack([one] * 3 + [-p for p in pieces] + [zero] * (HALF - 6), axis=1)
    r = jnp.arange(DA_TQ, dtype=jnp.int32)
    relu = jnp.maximum(r[:, None] - r[None, :], 0).astype(F32)
    return query_tab, key_tab, relu


def _sw_kernel(slopes_ref, sink_ref, q_ref, kk_ref, vv_ref, dist_ref, o_ref):
    j = pl.program_id(1)
    ch = pl.program_id(2)
    S = kk_ref.shape[0]
    nb = S // Q_BLOCK
    lane = lax.broadcasted_iota(jnp.int32, (Q_BLOCK, LANES), 1)
    low = lane < HALF
    dn = (((1,), (1,)), ((), ()))
    slope2 = [slopes_ref[j * SW_GROUP + g] * LOG2E for g in range(SW_GROUP)]
    sink2 = [sink_ref[j * SW_GROUP + g] * LOG2E for g in range(SW_GROUP)]

    for r in range(SW_BLOCKS_PER_STEP):
        n = ch * SW_BLOCKS_PER_STEP + r
        start = pl.multiple_of(jnp.clip((n - 1) * Q_BLOCK, 0, S - SW_WIN), Q_BLOCK)
        variant = jnp.where(n == 0, 0, jnp.where(n == nb - 1, 2, 1))
        dist = dist_ref[variant]
        kwin = kk_ref[pl.ds(start, SW_WIN), :]
        vwin = vv_ref[pl.ds(start, SW_WIN), :]
        rows = slice(r * Q_BLOCK, (r + 1) * Q_BLOCK)
        stacked = []
        for g in range(SW_GROUP):
            qpair = q_ref[rows, (g // 2) * LANES:(g // 2 + 1) * LANES]
            stacked.append(jnp.where(low if g % 2 == 0 else ~low, qpair, jnp.zeros_like(qpair)))
        s = lax.dot_general(jnp.concatenate(stacked, axis=0), kwin, dn,
                            preferred_element_type=F32)
        es, inv = [], []
        for g in range(SW_GROUP):
            sg = s[g * Q_BLOCK:(g + 1) * Q_BLOCK] - dist * slope2[g]
            m = jnp.maximum(jnp.max(sg, axis=-1, keepdims=True), sink2[g])
            e = jnp.exp2(sg - m)
            inv.append(1.0 / (jnp.sum(e, axis=-1, keepdims=True) + jnp.exp2(sink2[g] - m)))
            es.append(e.astype(BF16))
        o = jnp.dot(jnp.concatenate(es, axis=0), vwin, preferred_element_type=F32)
        og = [o[g * Q_BLOCK:(g + 1) * Q_BLOCK] * inv[g] for g in range(SW_GROUP)]
        for t in range(SW_GROUP // 2):
            o_ref[rows, t * LANES:(t + 1) * LANES] = jnp.where(low, og[2 * t], og[2 * t + 1]).astype(BF16)


def _sw_attention(q, kk, vv, dist, slopes, sink, *, B, S):
    T = q.shape[0]
    tq = SW_BLOCKS_PER_STEP * Q_BLOCK
    nq = S // tq
    gw = SW_GROUP * HALF
    return pl.pallas_call(
        _sw_kernel,
        out_shape=jax.ShapeDtypeStruct((T, SW_Q_HEADS * HALF), BF16),
        grid=(B, SW_KV_HEADS, nq),
        in_specs=[
            pl.BlockSpec(memory_space=pltpu.SMEM),
            pl.BlockSpec(memory_space=pltpu.SMEM),
            pl.BlockSpec((tq, gw), lambda b, j, n: (b * nq + n, j)),
            pl.BlockSpec((S, LANES), lambda b, j, n: (b, j)),
            pl.BlockSpec((S, LANES), lambda b, j, n: (b, j)),
            _const_spec(dist.shape),
        ],
        out_specs=pl.BlockSpec((tq, gw), lambda b, j, n: (b * nq + n, j)),
        compiler_params=_params(3),
        name="sw_attn",
    )(slopes, sink, q, kk, vv, dist)


def _sw_distance_table():
    shift = jnp.arange(3, dtype=jnp.int32)[:, None, None] * Q_BLOCK
    r = jnp.arange(Q_BLOCK, dtype=jnp.int32)[None, :, None]
    c = jnp.arange(SW_WIN, dtype=jnp.int32)[None, None, :]
    d = jnp.abs(shift + r - c)
    return jnp.where(d <= WINDOW, d.astype(F32), MASKED_DIST)


def _merge_kernel(x_ref, g_ref, wgate_ref, bg_ref, oda_ref, osw_ref, pa_ref, pb_ref, wo_ref, o_ref):
    x = x_ref[...]
    D = x.shape[1]
    hb = _rmsnorm(x, g_ref[...]).astype(BF16)
    gates = jnp.dot(hb, wgate_ref[...], preferred_element_type=F32) + bg_ref[...]
    gates = jax.nn.sigmoid(gates)
    ya = jnp.dot(oda_ref[...], pa_ref[...], preferred_element_type=F32)
    yb = jnp.dot(osw_ref[...], pb_ref[...], preferred_element_type=F32)
    merged = gates[:, :D] * ya + gates[:, D:] * yb
    o_ref[...] = x + jnp.dot(merged.astype(BF16), wo_ref[...], preferred_element_type=F32)


def _merge(x, g, wgate, bg, oda, osw, pa, pb, wo, *, tm=512):
    T, D = x.shape
    row = lambda i: (i, 0)
    return pl.pallas_call(
        _merge_kernel,
        out_shape=jax.ShapeDtypeStruct((T, D), F32),
        grid=(T // tm,),
        in_specs=[
            pl.BlockSpec((tm, D), row),
            _const_spec((1, D)),
            _const_spec((D, 2 * D)),
            _const_spec((1, 2 * D)),
            pl.BlockSpec((tm, D), row),
            pl.BlockSpec((tm, D), row),
            _const_spec((D, D)),
            _const_spec((D, D)),
            _const_spec((D, D)),
        ],
        out_specs=pl.BlockSpec((tm, D), row),
        compiler_params=_params(1),
        name="merge",
    )(x, g, wgate, bg, oda, osw, pa, pb, wo)


def _alibi_slopes(n):
    return 2.0 ** (-8.0 * jnp.arange(1, n + 1, dtype=F32) / n)


def kernel(x, norm_ffn1, ffn1_gate, ffn1_up, ffn1_down, norm_mix, w_in, b_gate, da_lambda, da_subnorm, swa_sink, w_proj_da, w_proj_swa, w_out, norm_ffn2, ffn2_gate, ffn2_up, ffn2_down, norm_final):
    B, S, D = x.shape
    depth = norm_ffn1.shape[0]
    assert depth >= 1, "the final RMSNorm is fused into the last layer's second FFN"
    T = B * S
    xt = x.reshape(T, D)
    bf = lambda a: a.astype(BF16)
    row = lambda a: a.reshape(1, -1)
    gf = row(norm_final)

    da_w = DA_HEADS * LANES
    swq_w = SW_Q_HEADS * HALF
    swkv_w = SW_KV_HEADS * HALF
    o_k, o_v = da_w, 2 * da_w
    o_sq = o_v + da_w
    o_sk = o_sq + swq_w
    o_sv = o_sk + swkv_w
    o_gate = o_sv + swkv_w

    def dup_heads(w):
        w = w.reshape(D, SW_KV_HEADS, 1, HALF)
        return jnp.broadcast_to(w, (D, SW_KV_HEADS, 2, HALF)).reshape(D, SW_KV_HEADS * LANES)

    widths = (da_w, swq_w, SW_KV_HEADS * LANES, SW_KV_HEADS * LANES)
    qscale = HALF ** -0.5 * LOG2E
    scales = (1.0, qscale, 1.0, 1.0)
    da_tables = _da_alibi_tables(S, _alibi_slopes(DA_HEADS))
    sw_dist = _sw_distance_table()

    for l in range(depth):
        lam_init = 0.8 - 0.6 * math.exp(-0.3 * l)
        wl = w_in[l]
        w_cols = bf(jnp.concatenate(
            [wl[:, o_k:o_v], wl[:, o_sq:o_sk], dup_heads(wl[:, o_sk:o_sv]), dup_heads(wl[:, o_sv:o_gate])],
            axis=1))
        w_t = bf(jnp.concatenate([wl[:, :o_k], wl[:, o_v:o_sq]], axis=1).T)

        xt = _ffn(xt, row(norm_ffn1[l]), bf(ffn1_gate[l]), bf(ffn1_up[l]), bf(ffn1_down[l]), gf,
                  final_norm=False)
        da_qt, da_vt, da_k, sw_q, sw_kk, sw_vv = _inproj(
            xt, row(norm_mix[l]), w_cols, w_t, widths, scales, qscale)
        o_da = _da_attention(da_qt, da_k, da_vt, da_tables, da_lambda[l], row(da_subnorm[l]),
                             _alibi_slopes(DA_HEADS), B=B, S=S, lam_init=lam_init)
        o_sw = _sw_attention(sw_q, sw_kk, sw_vv, sw_dist, _alibi_slopes(SW_Q_HEADS), swa_sink[l], B=B, S=S)
        xt = _merge(xt, row(norm_mix[l]), bf(wl[:, o_gate:]), row(b_gate[l]), o_da, o_sw,
                    bf(w_proj_da[l]), bf(w_proj_swa[l]), bf(w_out[l]))
        xt = _ffn(xt, row(norm_ffn2[l]), bf(ffn2_gate[l]), bf(ffn2_up[l]), bf(ffn2_down[l]), gf,
                  final_norm=(l == depth - 1))
    return xt.reshape(B, S, D)
```

```python
import functools
import math

import jax
import jax.numpy as jnp
from jax import lax
from jax.experimental import pallas as pl
from jax.experimental.pallas import tpu as pltpu

F32 = jnp.float32
BF16 = jnp.bfloat16

RMS_EPS = 1e-6
LOG2E = math.log2(math.e)
LANES = 128
HALF = 64
VMEM_LIMIT = 56 * 1024 * 1024
MASKED_DIST = 1e30

DA_HEADS = 8
DA_TQ = 256
DA_SUBBLOCKS = 4
DA_VAUG_ROWS = LANES + 16
DA_LAG = 2
DA_MAX_BOUND = 40.0
SW_Q_HEADS = 16
SW_KV_HEADS = 4
SW_GROUP = SW_Q_HEADS // SW_KV_HEADS
SW_BLOCKS_PER_STEP = 4
WINDOW = 128
Q_BLOCK = 128
SW_WIN = 3 * Q_BLOCK


def _rmsnorm(x, g):
    return x * lax.rsqrt(jnp.mean(x * x, axis=-1, keepdims=True) + RMS_EPS) * g


def _const_spec(shape):
    return pl.BlockSpec(shape, lambda *_: (0,) * len(shape), pipeline_mode=pl.Buffered(1))


def _params(n_axes):
    return pltpu.CompilerParams(
        dimension_semantics=("parallel",) * n_axes, vmem_limit_bytes=VMEM_LIMIT)


def _ffn_kernel(x_ref, g_ref, wg_ref, wu_ref, wd_ref, gf_ref, o_ref, a_scr, *, ff_chunk, final_norm):
    x = x_ref[...]
    hb = _rmsnorm(x, g_ref[...]).astype(BF16)
    d_ff = wg_ref.shape[1]
    for c in range(d_ff // ff_chunk):
        sl = slice(c * ff_chunk, (c + 1) * ff_chunk)
        gate = jnp.dot(hb, wg_ref[:, sl], preferred_element_type=F32)
        up = jnp.dot(hb, wu_ref[:, sl], preferred_element_type=F32)
        a_scr[:, sl] = (gate * jax.nn.sigmoid(gate) * up).astype(BF16)
    y = jnp.dot(a_scr[...], wd_ref[...], preferred_element_type=F32)
    x = x + 0.5 * y
    if final_norm:
        x = _rmsnorm(x, gf_ref[...])
    o_ref[...] = x


def _ffn(x, g, wg, wu, wd, gf, *, final_norm, tm=512, ff_chunk=256):
    T, D = x.shape
    d_ff = wg.shape[1]
    kern = functools.partial(_ffn_kernel, ff_chunk=ff_chunk, final_norm=final_norm)
    return pl.pallas_call(
        kern,
        out_shape=jax.ShapeDtypeStruct((T, D), F32),
        grid=(T // tm,),
        in_specs=[
            pl.BlockSpec((tm, D), lambda i: (i, 0)),
            _const_spec((1, D)),
            _const_spec((D, d_ff)),
            _const_spec((D, d_ff)),
            _const_spec((d_ff, D)),
            _const_spec((1, D)),
        ],
        out_specs=pl.BlockSpec((tm, D), lambda i: (i, 0)),
        scratch_shapes=[pltpu.VMEM((tm, d_ff), BF16)],
        compiler_params=_params(1),
        name="ffn_final" if final_norm else "ffn",
    )(x, g, wg, wu, wd, gf)


def _inproj_kernel(x_ref, g_ref, w_ref, wt_ref, qt_ref, vt_ref, *o_refs, widths, scales, qscale):
    hb = _rmsnorm(x_ref[...], g_ref[...]).astype(BF16)
    nt = (((1,), (1,)), ((), ()))
    da_w = qt_ref.shape[0]
    qt = lax.dot_general(wt_ref[:da_w, :], hb, nt, preferred_element_type=F32)
    qt_ref[...] = (qt * qscale).astype(BF16)
    vt = lax.dot_general(wt_ref[da_w:, :], hb, nt, preferred_element_type=F32)
    vt_ref[...] = vt.astype(BF16)
    off = 0
    for o_ref, width, scale in zip(o_refs, widths, scales):
        for c in range(0, width, 512):
            cw = min(512, width - c)
            y = jnp.dot(hb, w_ref[:, off + c:off + c + cw], preferred_element_type=F32)
            if scale != 1.0:
                y = y * scale
            o_ref[:, c:c + cw] = y.astype(BF16)
        off += width


def _inproj(x, g, w, wt, widths, scales, qscale, *, tm=512):
    T, D = x.shape
    da_w = wt.shape[0] // 2
    kern = functools.partial(_inproj_kernel, widths=widths, scales=scales, qscale=qscale)
    return pl.pallas_call(
        kern,
        out_shape=[jax.ShapeDtypeStruct((da_w, T), BF16)] * 2
        + [jax.ShapeDtypeStruct((T, wd), BF16) for wd in widths],
        grid=(T // tm,),
        in_specs=[
            pl.BlockSpec((tm, D), lambda i: (i, 0)),
            _const_spec((1, D)),
            _const_spec(w.shape),
            _const_spec(wt.shape),
        ],
        out_specs=[pl.BlockSpec((da_w, tm), lambda i: (0, i))] * 2
        + [pl.BlockSpec((tm, wd), lambda i: (i, 0)) for wd in widths],
        compiler_params=_params(1),
        name="in_proj",
    )(x, g, w, wt)


def _split3(x):
    def top_bits(y):
        return lax.bitcast_convert_type(lax.bitcast_convert_type(y, jnp.uint32) & jnp.uint32(0xFFFF0000), F32)
    hi = top_bits(x)
    mid = top_bits(x - hi)
    return [hi, mid, x - hi - mid]


def _da_kernel(slopes_ref, qt_ref, k_ref, vt_ref, dist_ref, lam_ref, sub_ref, o_ref,
               kaug_scr, vaug_scr, knorm_scr, stab_scr, *, lam_init):
    h = pl.program_id(1)
    step = pl.program_id(2)
    tq = DA_TQ
    S = k_ref.shape[0]
    nk = S // tq
    slope2 = slopes_ref[h] * LOG2E

    lp = lam_ref[...]
    lam = (jnp.exp(jnp.sum(lp[0:1] * lp[1:2], axis=-1, keepdims=True))
           - jnp.exp(jnp.sum(lp[2:3] * lp[3:4], axis=-1, keepdims=True)) + lam_init)

    @pl.when(step == 0)
    def _():
        k = k_ref[...]
        lane = lax.broadcasted_iota(jnp.int32, (S, LANES), 1)
        low = lane < HALF
        ones3 = jnp.where((lane % HALF) < 3, 1.0, 0.0).astype(BF16)
        kaug_scr[0] = jnp.where(low, k, ones3)
        kaug_scr[1] = jnp.where(low, ones3, k)
        kf = k.astype(F32)
        ksq = kf * kf
        for c in range(2):
            n2 = jnp.sum(jnp.where(low if c == 0 else ~low, ksq, 0.0), axis=1, keepdims=True)
            knorm_scr[c] = jnp.broadcast_to(jnp.max(n2, axis=0, keepdims=True), (1, LANES))
        vaug_scr[:LANES, :] = vt_ref[...]
        first = lax.broadcasted_iota(jnp.int32, (DA_VAUG_ROWS - LANES, S), 0) == 0
        vaug_scr[LANES:, :] = jnp.where(first, 1.0, 0.0).astype(BF16)

    row = lax.broadcasted_iota(jnp.int32, (HALF, tq), 0)

    def q_halves(sub):
        qt = qt_ref[:, sub * tq:(sub + 1) * tq]
        return qt[:HALF], qt[HALF:]

    def bias_tile(qi, J):
        return dist_ref[J - qi + (nk - 1)] * slope2

    bounds = []
    for sub in range(DA_SUBBLOCKS):
        for c, qh in enumerate(q_halves(sub)):
            qf = qh.astype(F32)
            qn2 = jnp.sum(qf * qf, axis=0, keepdims=True)
            bounds.append(jnp.sqrt(qn2 * knorm_scr[c][:, :1]) * 1.001 + 1e-3)
    worst = bounds[0]
    for b in bounds[1:]:
        worst = jnp.maximum(worst, b)
    loose = jnp.max(worst) > DA_MAX_BOUND

    @pl.when(jnp.logical_not(loose))
    def _():
        for i, b in enumerate(bounds):
            stab_scr[i] = b

    @pl.when(loose)
    def _():
        k = k_ref[...]
        for sub in range(DA_SUBBLOCKS):
            qi = step * DA_SUBBLOCKS + sub
            bias = jnp.concatenate([bias_tile(qi, J) for J in range(nk)], axis=0)
            q1, q2 = q_halves(sub)
            zero = jnp.zeros_like(q1)
            for c, qm in enumerate((jnp.concatenate([q1, zero], axis=0), jnp.concatenate([zero, q2], axis=0))):
                s = jnp.dot(k, qm, preferred_element_type=F32) - bias
                stab_scr[2 * sub + c] = jnp.max(s, axis=0, keepdims=True)

    def start(sub):
        halves = q_halves(sub)
        qm = []
        for c in range(2):
            p = _split3(-stab_scr[2 * sub + c])
            spare = jnp.where(row == 0, p[0], jnp.where(row == 1, p[1], jnp.where(row == 2, p[2], 0.0)))
            spare = spare.astype(BF16)
            qm.append(jnp.concatenate([halves[0], spare] if c == 0 else [spare, halves[1]], axis=0))
        return dict(sub=sub, qi=step * DA_SUBBLOCKS + sub, qm=qm, e={}, acc=[None, None])

    def score_piece(st, J):
        keys = slice(J * tq, (J + 1) * tq)
        bias = bias_tile(st["qi"], J)
        for c in range(2):
            s = jnp.dot(kaug_scr[c, keys, :], st["qm"][c], preferred_element_type=F32)
            st["e"][J, c] = jnp.exp2(s - bias).astype(BF16)

    def value_piece(st, J):
        keys = slice(J * tq, (J + 1) * tq)
        for c in range(2):
            pv = jnp.dot(vaug_scr[:, keys], st["e"].pop((J, c)), preferred_element_type=F32)
            st["acc"][c] = pv if J == 0 else st["acc"][c] + pv

    def finish(st):
        (o1, l1), (o2, l2) = ((a[:LANES], a[LANES:LANES + 1]) for a in st["acc"])
        ot = o1 * (1.0 / l1) - o2 * (lam / l2)
        o = _rmsnorm(ot.T, sub_ref[...]) * (1.0 - lam_init)
        o_ref[st["sub"] * tq:(st["sub"] + 1) * tq, :] = o.astype(BF16)

    blocks = {}
    n_pieces = DA_SUBBLOCKS * nk
    for g in range(n_pieces + DA_LAG):
        if g < n_pieces:
            if g % nk == 0:
                blocks[g // nk] = start(g // nk)
            score_piece(blocks[g // nk], g % nk)
        v = g - DA_LAG
        if v >= 0:
            value_piece(blocks[v // nk], v % nk)
            if v % nk == nk - 1:
                finish(blocks.pop(v // nk))


def _da_attention(qt, k, vt, dist, da_lambda, subnorm, slopes, *, B, S, lam_init):
    T = k.shape[0]
    tq = DA_TQ * DA_SUBBLOCKS
    nq = S // tq
    kern = functools.partial(_da_kernel, lam_init=lam_init)
    return pl.pallas_call(
        kern,
        out_shape=jax.ShapeDtypeStruct((T, DA_HEADS * LANES), BF16),
        grid=(B, DA_HEADS, nq),
        in_specs=[
            pl.BlockSpec(memory_space=pltpu.SMEM),
            pl.BlockSpec((LANES, tq), lambda b, h, i: (h, b * nq + i)),
            pl.BlockSpec((S, LANES), lambda b, h, i: (b, h)),
            pl.BlockSpec((LANES, S), lambda b, h, i: (h, b)),
            _const_spec(dist.shape),
            _const_spec((4, HALF)),
            _const_spec((1, LANES)),
        ],
        out_specs=pl.BlockSpec((tq, LANES), lambda b, h, i: (b * nq + i, h)),
        scratch_shapes=[
            pltpu.VMEM((2, S, LANES), BF16),
            pltpu.VMEM((DA_VAUG_ROWS, S), BF16),
            pltpu.VMEM((2, 1, LANES), F32),
            pltpu.VMEM((2 * DA_SUBBLOCKS, 1, DA_TQ), F32),
        ],
        compiler_params=pltpu.CompilerParams(
            dimension_semantics=("parallel", "parallel", "arbitrary"), vmem_limit_bytes=VMEM_LIMIT),
        name="da_attn",
    )(slopes, qt, k, vt, dist, da_lambda, subnorm)


def _da_distance_table(S):
    nk = S // DA_TQ
    t = jnp.arange(2 * nk - 1, dtype=jnp.int32)[:, None, None] - (nk - 1)
    r = jnp.arange(DA_TQ, dtype=jnp.int32)[None, :, None]
    m = jnp.arange(DA_TQ, dtype=jnp.int32)[None, None, :]
    return jnp.abs(t * DA_TQ + r - m).astype(F32)


def _sw_kernel(slopes_ref, sink_ref, q_ref, kk_ref, vv_ref, dist_ref, o_ref):
    j = pl.program_id(1)
    ch = pl.program_id(2)
    S = kk_ref.shape[0]
    nb = S // Q_BLOCK
    lane = lax.broadcasted_iota(jnp.int32, (Q_BLOCK, LANES), 1)
    low = lane < HALF
    dn = (((1,), (1,)), ((), ()))
    slope2 = [slopes_ref[j * SW_GROUP + g] * LOG2E for g in range(SW_GROUP)]
    sink2 = [sink_ref[j * SW_GROUP + g] * LOG2E for g in range(SW_GROUP)]

    for r in range(SW_BLOCKS_PER_STEP):
        n = ch * SW_BLOCKS_PER_STEP + r
        start = pl.multiple_of(jnp.clip((n - 1) * Q_BLOCK, 0, S - SW_WIN), Q_BLOCK)
        variant = jnp.where(n == 0, 0, jnp.where(n == nb - 1, 2, 1))
        dist = dist_ref[variant]
        kwin = kk_ref[pl.ds(start, SW_WIN), :]
        vwin = vv_ref[pl.ds(start, SW_WIN), :]
        rows = slice(r * Q_BLOCK, (r + 1) * Q_BLOCK)
        stacked = []
        for g in range(SW_GROUP):
            qpair = q_ref[rows, (g // 2) * LANES:(g // 2 + 1) * LANES]
            stacked.append(jnp.where(low if g % 2 == 0 else ~low, qpair, jnp.zeros_like(qpair)))
        s = lax.dot_general(jnp.concatenate(stacked, axis=0), kwin, dn,
                            preferred_element_type=F32)
        es, inv = [], []
        for g in range(SW_GROUP):
            sg = s[g * Q_BLOCK:(g + 1) * Q_BLOCK] - dist * slope2[g]
            m = jnp.maximum(jnp.max(sg, axis=-1, keepdims=True), sink2[g])
            e = jnp.exp2(sg - m)
            inv.append(1.0 / (jnp.sum(e, axis=-1, keepdims=True) + jnp.exp2(sink2[g] - m)))
            es.append(e.astype(BF16))
        o = jnp.dot(jnp.concatenate(es, axis=0), vwin, preferred_element_type=F32)
        og = [o[g * Q_BLOCK:(g + 1) * Q_BLOCK] * inv[g] for g in range(SW_GROUP)]
        for t in range(SW_GROUP // 2):
            o_ref[rows, t * LANES:(t + 1) * LANES] = jnp.where(low, og[2 * t], og[2 * t + 1]).astype(BF16)


def _sw_attention(q, kk, vv, dist, slopes, sink, *, B, S):
    T = q.shape[0]
    tq = SW_BLOCKS_PER_STEP * Q_BLOCK
    nq = S // tq
    gw = SW_GROUP * HALF
    return pl.pallas_call(
        _sw_kernel,
        out_shape=jax.ShapeDtypeStruct((T, SW_Q_HEADS * HALF), BF16),
        grid=(B, SW_KV_HEADS, nq),
        in_specs=[
            pl.BlockSpec(memory_space=pltpu.SMEM),
            pl.BlockSpec(memory_space=pltpu.SMEM),
            pl.BlockSpec((tq, gw), lambda b, j, n: (b * nq + n, j)),
            pl.BlockSpec((S, LANES), lambda b, j, n: (b, j)),
            pl.BlockSpec((S, LANES), lambda b, j, n: (b, j)),
            _const_spec(dist.shape),
        ],
        out_specs=pl.BlockSpec((tq, gw), lambda b, j, n: (b * nq + n, j)),
        compiler_params=_params(3),
        name="sw_attn",
    )(slopes, sink, q, kk, vv, dist)


def _sw_distance_table():
    shift = jnp.arange(3, dtype=jnp.int32)[:, None, None] * Q_BLOCK
    r = jnp.arange(Q_BLOCK, dtype=jnp.int32)[None, :, None]
    c = jnp.arange(SW_WIN, dtype=jnp.int32)[None, None, :]
    d = jnp.abs(shift + r - c)
    return jnp.where(d <= WINDOW, d.astype(F32), MASKED_DIST)


def _merge_kernel(x_ref, g_ref, wgate_ref, bg_ref, oda_ref, osw_ref, pa_ref, pb_ref, wo_ref, o_ref):
    x = x_ref[...]
    D = x.shape[1]
    hb = _rmsnorm(x, g_ref[...]).astype(BF16)
    gates = jnp.dot(hb, wgate_ref[...], preferred_element_type=F32) + bg_ref[...]
    gates = jax.nn.sigmoid(gates)
    ya = jnp.dot(oda_ref[...], pa_ref[...], preferred_element_type=F32)
    yb = jnp.dot(osw_ref[...], pb_ref[...], preferred_element_type=F32)
    merged = gates[:, :D] * ya + gates[:, D:] * yb
    o_ref[...] = x + jnp.dot(merged.astype(BF16), wo_ref[...], preferred_element_type=F32)


def _merge(x, g, wgate, bg, oda, osw, pa, pb, wo, *, tm=512):
    T, D = x.shape
    row = lambda i: (i, 0)
    return pl.pallas_call(
        _merge_kernel,
        out_shape=jax.ShapeDtypeStruct((T, D), F32),
        grid=(T // tm,),
        in_specs=[
            pl.BlockSpec((tm, D), row),
            _const_spec((1, D)),
            _const_spec((D, 2 * D)),
            _const_spec((1, 2 * D)),
            pl.BlockSpec((tm, D), row),
            pl.BlockSpec((tm, D), row),
            _const_spec((D, D)),
            _const_spec((D, D)),
            _const_spec((D, D)),
        ],
        out_specs=pl.BlockSpec((tm, D), row),
        compiler_params=_params(1),
        name="merge",
    )(x, g, wgate, bg, oda, osw, pa, pb, wo)


def _alibi_slopes(n):
    return 2.0 ** (-8.0 * jnp.arange(1, n + 1, dtype=F32) / n)


def kernel(x, norm_ffn1, ffn1_gate, ffn1_up, ffn1_down, norm_mix, w_in, b_gate, da_lambda, da_subnorm, swa_sink, w_proj_da, w_proj_swa, w_out, norm_ffn2, ffn2_gate, ffn2_up, ffn2_down, norm_final):
    B, S, D = x.shape
    depth = norm_ffn1.shape[0]
    assert depth >= 1, "the final RMSNorm is fused into the last layer's second FFN"
    T = B * S
    xt = x.reshape(T, D)
    bf = lambda a: a.astype(BF16)
    row = lambda a: a.reshape(1, -1)
    gf = row(norm_final)

    da_w = DA_HEADS * LANES
    swq_w = SW_Q_HEADS * HALF
    swkv_w = SW_KV_HEADS * HALF
    o_k, o_v = da_w, 2 * da_w
    o_sq = o_v + da_w
    o_sk = o_sq + swq_w
    o_sv = o_sk + swkv_w
    o_gate = o_sv + swkv_w

    def dup_heads(w):
        w = w.reshape(D, SW_KV_HEADS, 1, HALF)
        return jnp.broadcast_to(w, (D, SW_KV_HEADS, 2, HALF)).reshape(D, SW_KV_HEADS * LANES)

    widths = (da_w, swq_w, SW_KV_HEADS * LANES, SW_KV_HEADS * LANES)
    qscale = HALF ** -0.5 * LOG2E
    scales = (1.0, qscale, 1.0, 1.0)
    da_dist = _da_distance_table(S)
    sw_dist = _sw_distance_table()

    for l in range(depth):
        lam_init = 0.8 - 0.6 * math.exp(-0.3 * l)
        wl = w_in[l]
        w_cols = bf(jnp.concatenate(
            [wl[:, o_k:o_v], wl[:, o_sq:o_sk], dup_heads(wl[:, o_sk:o_sv]), dup_heads(wl[:, o_sv:o_gate])],
            axis=1))
        w_t = bf(jnp.concatenate([wl[:, :o_k], wl[:, o_v:o_sq]], axis=1).T)

        xt = _ffn(xt, row(norm_ffn1[l]), bf(ffn1_gate[l]), bf(ffn1_up[l]), bf(ffn1_down[l]), gf,
                  final_norm=False)
        da_qt, da_vt, da_k, sw_q, sw_kk, sw_vv = _inproj(
            xt, row(norm_mix[l]), w_cols, w_t, widths, scales, qscale)
        o_da = _da_attention(da_qt, da_k, da_vt, da_dist, da_lambda[l], row(da_subnorm[l]),
                             _alibi_slopes(DA_HEADS), B=B, S=S, lam_init=lam_init)
        o_sw = _sw_attention(sw_q, sw_kk, sw_vv, sw_dist, _alibi_slopes(SW_Q_HEADS), swa_sink[l], B=B, S=S)
        xt = _merge(xt, row(norm_mix[l]), bf(wl[:, o_gate:]), row(b_gate[l]), o_da, o_sw,
                    bf(w_proj_da[l]), bf(w_proj_swa[l]), bf(w_out[l]))
        xt = _ffn(xt, row(norm_ffn2[l]), bf(ffn2_gate[l]), bf(ffn2_up[l]), bf(ffn2_down[l]), gf,
                  final_norm=(l == depth - 1))
    return xt.reshape(B, S, D)
```

```python
import functools
import math

import jax
import jax.numpy as jnp
from jax import lax
from jax.experimental import pallas as pl
from jax.experimental.pallas import tpu as pltpu

F32 = jnp.float32
BF16 = jnp.bfloat16

RMS_EPS = 1e-6
LOG2E = math.log2(math.e)
LANES = 128
HALF = 64
VMEM_LIMIT = 56 * 1024 * 1024
MASKED_DIST = 1e30

DA_HEADS = 8
DA_TQ = 256
DA_SUBBLOCKS = 4
DA_VAUG_ROWS = LANES + 16
DA_LAG = 2
DA_MAX_BOUND = 40.0
SW_Q_HEADS = 16
SW_KV_HEADS = 4
SW_GROUP = SW_Q_HEADS // SW_KV_HEADS
SW_BLOCKS_PER_STEP = 8
WINDOW = 128
Q_BLOCK = 128
SW_WIN = 3 * Q_BLOCK
SW_VAUG_ROWS = HALF + 16


def _rmsnorm(x, g):
    return x * lax.rsqrt(jnp.mean(x * x, axis=-1, keepdims=True) + RMS_EPS) * g


def _const_spec(shape):
    return pl.BlockSpec(shape, lambda *_: (0,) * len(shape), pipeline_mode=pl.Buffered(1))


def _params(n_axes):
    return pltpu.CompilerParams(
        dimension_semantics=("parallel",) * n_axes, vmem_limit_bytes=VMEM_LIMIT)


def _ffn_kernel(x_ref, g_ref, wg_ref, wu_ref, wd_ref, gf_ref, o_ref, a_scr, *, ff_chunk, final_norm):
    x = x_ref[...]
    hb = _rmsnorm(x, g_ref[...]).astype(BF16)
    d_ff = wg_ref.shape[1]
    for c in range(d_ff // ff_chunk):
        sl = slice(c * ff_chunk, (c + 1) * ff_chunk)
        gate = jnp.dot(hb, wg_ref[:, sl], preferred_element_type=F32)
        up = jnp.dot(hb, wu_ref[:, sl], preferred_element_type=F32)
        a_scr[:, sl] = (gate * jax.nn.sigmoid(gate) * up).astype(BF16)
    y = jnp.dot(a_scr[...], wd_ref[...], preferred_element_type=F32)
    x = x + 0.5 * y
    if final_norm:
        x = _rmsnorm(x, gf_ref[...])
    o_ref[...] = x


def _ffn(x, g, wg, wu, wd, gf, *, final_norm, tm=512, ff_chunk=256):
    T, D = x.shape
    d_ff = wg.shape[1]
    kern = functools.partial(_ffn_kernel, ff_chunk=ff_chunk, final_norm=final_norm)
    return pl.pallas_call(
        kern,
        out_shape=jax.ShapeDtypeStruct((T, D), F32),
        grid=(T // tm,),
        in_specs=[
            pl.BlockSpec((tm, D), lambda i: (i, 0)),
            _const_spec((1, D)),
            _const_spec((D, d_ff)),
            _const_spec((D, d_ff)),
            _const_spec((d_ff, D)),
            _const_spec((1, D)),
        ],
        out_specs=pl.BlockSpec((tm, D), lambda i: (i, 0)),
        scratch_shapes=[pltpu.VMEM((tm, d_ff), BF16)],
        compiler_params=_params(1),
        name="ffn_final" if final_norm else "ffn",
    )(x, g, wg, wu, wd, gf)


def _inproj_kernel(x_ref, g_ref, w_ref, wt_ref, *o_refs, t_rows, t_scales, widths):
    hb = _rmsnorm(x_ref[...], g_ref[...]).astype(BF16)
    nt = (((1,), (1,)), ((), ()))
    off = 0
    for o_ref, rows, scale in zip(o_refs, t_rows, t_scales):
        for c in range(0, rows, 512):
            cw = min(512, rows - c)
            y = lax.dot_general(wt_ref[off + c:off + c + cw, :], hb, nt, preferred_element_type=F32)
            o_ref[c:c + cw, :] = (y * scale if scale != 1.0 else y).astype(BF16)
        off += rows
    off = 0
    for o_ref, width in zip(o_refs[len(t_rows):], widths):
        for c in range(0, width, 512):
            cw = min(512, width - c)
            y = jnp.dot(hb, w_ref[:, off + c:off + c + cw], preferred_element_type=F32)
            o_ref[:, c:c + cw] = y.astype(BF16)
        off += width


def _inproj(x, g, w, wt, t_rows, t_scales, widths, *, tm=512):
    T, D = x.shape
    kern = functools.partial(_inproj_kernel, t_rows=t_rows, t_scales=t_scales, widths=widths)
    return pl.pallas_call(
        kern,
        out_shape=[jax.ShapeDtypeStruct((rows, T), BF16) for rows in t_rows]
        + [jax.ShapeDtypeStruct((T, wd), BF16) for wd in widths],
        grid=(T // tm,),
        in_specs=[
            pl.BlockSpec((tm, D), lambda i: (i, 0)),
            _const_spec((1, D)),
            _const_spec(w.shape),
            _const_spec(wt.shape),
        ],
        out_specs=[pl.BlockSpec((rows, tm), lambda i: (0, i)) for rows in t_rows]
        + [pl.BlockSpec((tm, wd), lambda i: (i, 0)) for wd in widths],
        compiler_params=_params(1),
        name="in_proj",
    )(x, g, w, wt)


def _split3(x):
    def top_bits(y):
        return lax.bitcast_convert_type(lax.bitcast_convert_type(y, jnp.uint32) & jnp.uint32(0xFFFF0000), F32)
    hi = top_bits(x)
    mid = top_bits(x - hi)
    return [hi, mid, x - hi - mid]


def _da_kernel(slopes_ref, qt_ref, k_ref, vt_ref, dist_ref, lam_ref, sub_ref, o_ref,
               kaug_scr, vaug_scr, knorm_scr, stab_scr, *, lam_init):
    h = pl.program_id(1)
    step = pl.program_id(2)
    tq = DA_TQ
    S = k_ref.shape[0]
    nk = S // tq
    slope2 = slopes_ref[h] * LOG2E

    lp = lam_ref[...]
    lam = (jnp.exp(jnp.sum(lp[0:1] * lp[1:2], axis=-1, keepdims=True))
           - jnp.exp(jnp.sum(lp[2:3] * lp[3:4], axis=-1, keepdims=True)) + lam_init)

    @pl.when(step == 0)
    def _():
        k = k_ref[...]
        lane = lax.broadcasted_iota(jnp.int32, (S, LANES), 1)
        low = lane < HALF
        ones3 = jnp.where((lane % HALF) < 3, 1.0, 0.0).astype(BF16)
        kaug_scr[0] = jnp.where(low, k, ones3)
        kaug_scr[1] = jnp.where(low, ones3, k)
        kf = k.astype(F32)
        ksq = kf * kf
        for c in range(2):
            n2 = jnp.sum(jnp.where(low if c == 0 else ~low, ksq, 0.0), axis=1, keepdims=True)
            knorm_scr[c] = jnp.broadcast_to(jnp.max(n2, axis=0, keepdims=True), (1, LANES))
        vaug_scr[:LANES, :] = vt_ref[...]
        first = lax.broadcasted_iota(jnp.int32, (DA_VAUG_ROWS - LANES, S), 0) == 0
        vaug_scr[LANES:, :] = jnp.where(first, 1.0, 0.0).astype(BF16)

    row = lax.broadcasted_iota(jnp.int32, (HALF, tq), 0)

    def q_halves(sub):
        qt = qt_ref[:, sub * tq:(sub + 1) * tq]
        return qt[:HALF], qt[HALF:]

    def bias_tile(qi, J):
        return dist_ref[J - qi + (nk - 1)] * slope2

    bounds = []
    for sub in range(DA_SUBBLOCKS):
        for c, qh in enumerate(q_halves(sub)):
            qf = qh.astype(F32)
            qn2 = jnp.sum(qf * qf, axis=0, keepdims=True)
            bounds.append(jnp.sqrt(qn2 * knorm_scr[c][:, :1]) * 1.001 + 1e-3)
    worst = bounds[0]
    for b in bounds[1:]:
        worst = jnp.maximum(worst, b)
    loose = jnp.max(worst) > DA_MAX_BOUND

    @pl.when(jnp.logical_not(loose))
    def _():
        for i, b in enumerate(bounds):
            stab_scr[i] = b

    @pl.when(loose)
    def _():
        k = k_ref[...]
        for sub in range(DA_SUBBLOCKS):
            qi = step * DA_SUBBLOCKS + sub
            bias = jnp.concatenate([bias_tile(qi, J) for J in range(nk)], axis=0)
            q1, q2 = q_halves(sub)
            zero = jnp.zeros_like(q1)
            for c, qm in enumerate((jnp.concatenate([q1, zero], axis=0), jnp.concatenate([zero, q2], axis=0))):
                s = jnp.dot(k, qm, preferred_element_type=F32) - bias
                stab_scr[2 * sub + c] = jnp.max(s, axis=0, keepdims=True)

    def start(sub):
        halves = q_halves(sub)
        qm = []
        for c in range(2):
            p = _split3(-stab_scr[2 * sub + c])
            spare = jnp.where(row == 0, p[0], jnp.where(row == 1, p[1], jnp.where(row == 2, p[2], 0.0)))
            spare = spare.astype(BF16)
            qm.append(jnp.concatenate([halves[0], spare] if c == 0 else [spare, halves[1]], axis=0))
        return dict(sub=sub, qi=step * DA_SUBBLOCKS + sub, qm=qm, e={}, acc=[None, None])

    def score_piece(st, J):
        keys = slice(J * tq, (J + 1) * tq)
        bias = bias_tile(st["qi"], J)
        for c in range(2):
            s = jnp.dot(kaug_scr[c, keys, :], st["qm"][c], preferred_element_type=F32)
            st["e"][J, c] = jnp.exp2(s - bias).astype(BF16)

    def value_piece(st, J):
        keys = slice(J * tq, (J + 1) * tq)
        for c in range(2):
            pv = jnp.dot(vaug_scr[:, keys], st["e"].pop((J, c)), preferred_element_type=F32)
            st["acc"][c] = pv if J == 0 else st["acc"][c] + pv

    def finish(st):
        (o1, l1), (o2, l2) = ((a[:LANES], a[LANES:LANES + 1]) for a in st["acc"])
        ot = o1 * (1.0 / l1) - o2 * (lam / l2)
        o = _rmsnorm(ot.T, sub_ref[...]) * (1.0 - lam_init)
        o_ref[st["sub"] * tq:(st["sub"] + 1) * tq, :] = o.astype(BF16)

    blocks = {}
    n_pieces = DA_SUBBLOCKS * nk
    for g in range(n_pieces + DA_LAG):
        if g < n_pieces:
            if g % nk == 0:
                blocks[g // nk] = start(g // nk)
            score_piece(blocks[g // nk], g % nk)
        v = g - DA_LAG
        if v >= 0:
            value_piece(blocks[v // nk], v % nk)
            if v % nk == nk - 1:
                finish(blocks.pop(v // nk))


def _da_attention(qt, k, vt, dist, da_lambda, subnorm, slopes, *, B, S, lam_init):
    T = k.shape[0]
    tq = DA_TQ * DA_SUBBLOCKS
    nq = S // tq
    kern = functools.partial(_da_kernel, lam_init=lam_init)
    return pl.pallas_call(
        kern,
        out_shape=jax.ShapeDtypeStruct((T, DA_HEADS * LANES), BF16),
        grid=(B, DA_HEADS, nq),
        in_specs=[
            pl.BlockSpec(memory_space=pltpu.SMEM),
            pl.BlockSpec((LANES, tq), lambda b, h, i: (h, b * nq + i)),
            pl.BlockSpec((S, LANES), lambda b, h, i: (b, h)),
            pl.BlockSpec((LANES, S), lambda b, h, i: (h, b)),
            _const_spec(dist.shape),
            _const_spec((4, HALF)),
            _const_spec((1, LANES)),
        ],
        out_specs=pl.BlockSpec((tq, LANES), lambda b, h, i: (b * nq + i, h)),
        scratch_shapes=[
            pltpu.VMEM((2, S, LANES), BF16),
            pltpu.VMEM((DA_VAUG_ROWS, S), BF16),
            pltpu.VMEM((2, 1, LANES), F32),
            pltpu.VMEM((2 * DA_SUBBLOCKS, 1, DA_TQ), F32),
        ],
        compiler_params=pltpu.CompilerParams(
            dimension_semantics=("parallel", "parallel", "arbitrary"), vmem_limit_bytes=VMEM_LIMIT),
        name="da_attn",
    )(slopes, qt, k, vt, dist, da_lambda, subnorm)


def _da_distance_table(S):
    nk = S // DA_TQ
    t = jnp.arange(2 * nk - 1, dtype=jnp.int32)[:, None, None] - (nk - 1)
    r = jnp.arange(DA_TQ, dtype=jnp.int32)[None, :, None]
    m = jnp.arange(DA_TQ, dtype=jnp.int32)[None, None, :]
    return jnp.abs(t * DA_TQ + r - m).astype(F32)


def _sw_kernel(sink_ref, qt_ref, kk_ref, vt_ref, bias_ref, o_ref, kaug_scr, vaug_scr, knorm_scr):
    j = pl.program_id(1)
    ch = pl.program_id(2)
    S = kk_ref.shape[0]
    nb = S // Q_BLOCK
    wide = SW_GROUP * Q_BLOCK
    sink2 = jnp.concatenate(
        [jnp.full((1, Q_BLOCK), sink_ref[j * SW_GROUP + g] * LOG2E, F32) for g in range(SW_GROUP)], axis=1)

    @pl.when(ch == 0)
    def _():
        kk = kk_ref[...]
        lane = lax.broadcasted_iota(jnp.int32, (S, LANES), 1)
        low = lane < HALF
        kaug_scr[...] = jnp.where(low, kk, jnp.where(lane < HALF + 3, 1.0, 0.0).astype(BF16))
        kf = kk.astype(F32)
        n2 = jnp.sum(jnp.where(low, kf * kf, 0.0), axis=1, keepdims=True)
        knorm_scr[...] = jnp.broadcast_to(jnp.max(n2, axis=0, keepdims=True), (1, LANES))
        first = lax.broadcasted_iota(jnp.int32, (SW_VAUG_ROWS - HALF, Q_BLOCK), 0) == 0
        for n in range(nb):
            vaug_scr[n, :HALF, :] = vt_ref[:, n * Q_BLOCK:(n + 1) * Q_BLOCK]
            vaug_scr[n, HALF:, :] = jnp.where(first, 1.0, 0.0).astype(BF16)

    row = lax.broadcasted_iota(jnp.int32, (HALF, wide), 0)

    def block(r):
        n = ch * SW_BLOCKS_PER_STEP + r
        first_blk = jnp.clip(n - 1, 0, nb - 3)
        variant = jnp.where(n == 0, 0, jnp.where(n == nb - 1, 2, 1))
        qt4 = qt_ref[:, r * Q_BLOCK:(r + 1) * Q_BLOCK]
        q_wide = jnp.concatenate([qt4[g * HALF:(g + 1) * HALF] for g in range(SW_GROUP)], axis=1)
        return dict(r=r, first_blk=first_blk, variant=variant, q_wide=q_wide)

    def key_window(blk):
        return kaug_scr[pl.ds(pl.multiple_of(blk["first_blk"] * Q_BLOCK, Q_BLOCK), SW_WIN), :]

    blocks = [block(r) for r in range(SW_BLOCKS_PER_STEP)]

    def scores(blk, m):
        p = _split3(-m)
        spare = jnp.where(row == 0, p[0], jnp.where(row == 1, p[1], jnp.where(row == 2, p[2], 0.0))).astype(BF16)
        qm = jnp.concatenate([blk["q_wide"], spare], axis=0)
        s = jnp.dot(key_window(blk), qm, preferred_element_type=F32)
        return jnp.exp2(s - bias_ref[blk["variant"]]).astype(BF16)

    def values(blk, m, e):
        fb = blk["first_blk"]
        vwin = jnp.concatenate([vaug_scr[fb], vaug_scr[fb + 1], vaug_scr[fb + 2]], axis=1)
        ov = jnp.dot(vwin, e, preferred_element_type=F32)
        denom = ov[HALF:HALF + 1] + jnp.exp2(sink2 - m)
        on = ov[:HALF] * (1.0 / denom)
        rows = slice(blk["r"] * Q_BLOCK, (blk["r"] + 1) * Q_BLOCK)
        for t in range(SW_GROUP // 2):
            pair = jnp.concatenate([on[:, (2 * t) * Q_BLOCK:(2 * t + 1) * Q_BLOCK],
                                    on[:, (2 * t + 1) * Q_BLOCK:(2 * t + 2) * Q_BLOCK]], axis=0)
            o_ref[rows, t * LANES:(t + 1) * LANES] = pair.T.astype(BF16)

    def stream(stabilisers):
        e = {}
        for r in range(SW_BLOCKS_PER_STEP + 1):
            if r < SW_BLOCKS_PER_STEP:
                e[r] = scores(blocks[r], stabilisers[r])
            if r >= 1:
                values(blocks[r - 1], stabilisers[r - 1], e.pop(r - 1))

    bounds = []
    for blk in blocks:
        qf = blk["q_wide"].astype(F32)
        qn2 = jnp.sum(qf * qf, axis=0, keepdims=True)
        bounds.append(jnp.sqrt(qn2 * knorm_scr[:, :1]) * 1.001 + 1e-3)
    stream([jnp.maximum(b, sink2) for b in bounds])

    worst = bounds[0]
    for b in bounds[1:]:
        worst = jnp.maximum(worst, b)

    @pl.when(jnp.max(worst) > DA_MAX_BOUND)
    def _():
        exact = []
        for blk in blocks:
            qm = jnp.concatenate([blk["q_wide"], jnp.zeros_like(blk["q_wide"])], axis=0)
            s = jnp.dot(key_window(blk), qm, preferred_element_type=F32) - bias_ref[blk["variant"]]
            exact.append(jnp.maximum(jnp.max(s, axis=0, keepdims=True), sink2))
        stream(exact)


def _sw_attention(qt, kk, vt, bias, sink, *, B, S):
    T = kk.shape[0]
    tq = SW_BLOCKS_PER_STEP * Q_BLOCK
    nq = S // tq
    nb = S // Q_BLOCK
    gw = SW_GROUP * HALF
    return pl.pallas_call(
        _sw_kernel,
        out_shape=jax.ShapeDtypeStruct((T, SW_Q_HEADS * HALF), BF16),
        grid=(B, SW_KV_HEADS, nq),
        in_specs=[
            pl.BlockSpec(memory_space=pltpu.SMEM),
            pl.BlockSpec((gw, tq), lambda b, j, n: (j, b * nq + n)),
            pl.BlockSpec((S, LANES), lambda b, j, n: (b, j)),
            pl.BlockSpec((HALF, S), lambda b, j, n: (j, b)),
            pl.BlockSpec((None,) + bias.shape[1:], lambda b, j, n: (j, 0, 0, 0)),
        ],
        out_specs=pl.BlockSpec((tq, gw), lambda b, j, n: (b * nq + n, j)),
        scratch_shapes=[
            pltpu.VMEM((S, LANES), BF16),
            pltpu.VMEM((nb, SW_VAUG_ROWS, Q_BLOCK), BF16),
            pltpu.VMEM((1, LANES), F32),
        ],
        compiler_params=pltpu.CompilerParams(
            dimension_semantics=("parallel", "parallel", "arbitrary"), vmem_limit_bytes=VMEM_LIMIT),
        name="sw_attn",
    )(sink, qt, kk, vt, bias)


def _sw_bias_table(slopes):
    shift = jnp.arange(3, dtype=jnp.int32)[:, None, None] * Q_BLOCK
    c = jnp.arange(SW_WIN, dtype=jnp.int32)[None, :, None]
    r = jnp.arange(Q_BLOCK, dtype=jnp.int32)[None, None, :]
    d = jnp.abs(shift + r - c)
    dist = jnp.where(d <= WINDOW, d.astype(F32), MASKED_DIST)
    s2 = (slopes * LOG2E).reshape(SW_KV_HEADS, 1, 1, SW_GROUP, 1)
    table = jnp.minimum(dist[None, :, :, None, :] * s2, MASKED_DIST)
    return table.reshape(SW_KV_HEADS, 3, SW_WIN, SW_GROUP * Q_BLOCK)


def _merge_kernel(x_ref, g_ref, wgate_ref, bg_ref, oda_ref, osw_ref, pa_ref, pb_ref, wo_ref, o_ref):
    x = x_ref[...]
    D = x.shape[1]
    hb = _rmsnorm(x, g_ref[...]).astype(BF16)
    gates = jnp.dot(hb, wgate_ref[...], preferred_element_type=F32) + bg_ref[...]
    gates = jax.nn.sigmoid(gates)
    ya = jnp.dot(oda_ref[...], pa_ref[...], preferred_element_type=F32)
    yb = jnp.dot(osw_ref[...], pb_ref[...], preferred_element_type=F32)
    merged = gates[:, :D] * ya + gates[:, D:] * yb
    o_ref[...] = x + jnp.dot(merged.astype(BF16), wo_ref[...], preferred_element_type=F32)


def _merge(x, g, wgate, bg, oda, osw, pa, pb, wo, *, tm=512):
    T, D = x.shape
    row = lambda i: (i, 0)
    return pl.pallas_call(
        _merge_kernel,
        out_shape=jax.ShapeDtypeStruct((T, D), F32),
        grid=(T // tm,),
        in_specs=[
            pl.BlockSpec((tm, D), row),
            _const_spec((1, D)),
            _const_spec((D, 2 * D)),
            _const_spec((1, 2 * D)),
            pl.BlockSpec((tm, D), row),
            pl.BlockSpec((tm, D), row),
            _const_spec((D, D)),
            _const_spec((D, D)),
            _const_spec((D, D)),
        ],
        out_specs=pl.BlockSpec((tm, D), row),
        compiler_params=_params(1),
        name="merge",
    )(x, g, wgate, bg, oda, osw, pa, pb, wo)


def _alibi_slopes(n):
    return 2.0 ** (-8.0 * jnp.arange(1, n + 1, dtype=F32) / n)


def kernel(x, norm_ffn1, ffn1_gate, ffn1_up, ffn1_down, norm_mix, w_in, b_gate, da_lambda, da_subnorm, swa_sink, w_proj_da, w_proj_swa, w_out, norm_ffn2, ffn2_gate, ffn2_up, ffn2_down, norm_final):
    B, S, D = x.shape
    depth = norm_ffn1.shape[0]
    assert depth >= 1, "the final RMSNorm is fused into the last layer's second FFN"
    T = B * S
    xt = x.reshape(T, D)
    bf = lambda a: a.astype(BF16)
    row = lambda a: a.reshape(1, -1)
    gf = row(norm_final)

    da_w = DA_HEADS * LANES
    swq_w = SW_Q_HEADS * HALF
    swkv_w = SW_KV_HEADS * HALF
    o_k, o_v = da_w, 2 * da_w
    o_sq = o_v + da_w
    o_sk = o_sq + swq_w
    o_sv = o_sk + swkv_w
    o_gate = o_sv + swkv_w

    def dup_heads(w):
        w = w.reshape(D, SW_KV_HEADS, 1, HALF)
        return jnp.broadcast_to(w, (D, SW_KV_HEADS, 2, HALF)).reshape(D, SW_KV_HEADS * LANES)

    t_rows = (da_w, da_w, swq_w, swkv_w)
    qscale = HALF ** -0.5 * LOG2E
    t_scales = (qscale, 1.0, qscale, 1.0)
    widths = (da_w, SW_KV_HEADS * LANES)
    da_dist = _da_distance_table(S)
    sw_bias = _sw_bias_table(_alibi_slopes(SW_Q_HEADS))

    for l in range(depth):
        lam_init = 0.8 - 0.6 * math.exp(-0.3 * l)
        wl = w_in[l]
        w_cols = bf(jnp.concatenate([wl[:, o_k:o_v], dup_heads(wl[:, o_sk:o_sv])], axis=1))
        w_t = bf(jnp.concatenate([wl[:, :o_k], wl[:, o_v:o_sk], wl[:, o_sv:o_gate]], axis=1).T)

        xt = _ffn(xt, row(norm_ffn1[l]), bf(ffn1_gate[l]), bf(ffn1_up[l]), bf(ffn1_down[l]), gf,
                  final_norm=False)
        da_qt, da_vt, sw_qt, sw_vt, da_k, sw_kk = _inproj(
            xt, row(norm_mix[l]), w_cols, w_t, t_rows, t_scales, widths)
        o_da = _da_attention(da_qt, da_k, da_vt, da_dist, da_lambda[l], row(da_subnorm[l]),
                             _alibi_slopes(DA_HEADS), B=B, S=S, lam_init=lam_init)
        o_sw = _sw_attention(sw_qt, sw_kk, sw_vt, sw_bias, swa_sink[l], B=B, S=S)
        xt = _merge(xt, row(norm_mix[l]), bf(wl[:, o_gate:]), row(b_gate[l]), o_da, o_sw,
                    bf(w_proj_da[l]), bf(w_proj_swa[l]), bf(w_out[l]))
        xt = _ffn(xt, row(norm_ffn2[l]), bf(ffn2_gate[l]), bf(ffn2_up[l]), bf(ffn2_down[l]), gf,
                  final_norm=(l == depth - 1))
    return xt.reshape(B, S, D)
```

```python
import functools
import math

import jax
import jax.numpy as jnp
from jax import lax
from jax.experimental import pallas as pl
from jax.experimental.pallas import tpu as pltpu

F32 = jnp.float32
BF16 = jnp.bfloat16

RMS_EPS = 1e-6
LOG2E = math.log2(math.e)
LANES = 128
HALF = 64
VMEM_LIMIT = 56 * 1024 * 1024
MASKED_DIST = 1e30

DA_HEADS = 8
DA_TQ = 256
DA_SUBBLOCKS = 8
DA_VAUG_ROWS = LANES + 16
DA_LAG = 2
DA_MAX_BOUND = 40.0
SW_Q_HEADS = 16
SW_KV_HEADS = 4
SW_GROUP = SW_Q_HEADS // SW_KV_HEADS
SW_BLOCKS_PER_STEP = 8
WINDOW = 128
Q_BLOCK = 128
SW_WIN = 3 * Q_BLOCK
SW_VAUG_ROWS = HALF + 16


def _rmsnorm(x, g):
    return x * lax.rsqrt(jnp.mean(x * x, axis=-1, keepdims=True) + RMS_EPS) * g


def _const_spec(shape):
    return pl.BlockSpec(shape, lambda *_: (0,) * len(shape), pipeline_mode=pl.Buffered(1))


def _params(n_axes):
    return pltpu.CompilerParams(
        dimension_semantics=("parallel",) * n_axes, vmem_limit_bytes=VMEM_LIMIT)


def _ffn_kernel(x_ref, g_ref, wg_ref, wu_ref, wd_ref, gf_ref, o_ref, a_scr, *, ff_chunk, final_norm):
    x = x_ref[...]
    hb = _rmsnorm(x, g_ref[...]).astype(BF16)
    d_ff = wg_ref.shape[1]
    for c in range(d_ff // ff_chunk):
        sl = slice(c * ff_chunk, (c + 1) * ff_chunk)
        gate = jnp.dot(hb, wg_ref[:, sl], preferred_element_type=F32)
        up = jnp.dot(hb, wu_ref[:, sl], preferred_element_type=F32)
        a_scr[:, sl] = (gate * jax.nn.sigmoid(gate) * up).astype(BF16)
    y = jnp.dot(a_scr[...], wd_ref[...], preferred_element_type=F32)
    x = x + 0.5 * y
    if final_norm:
        x = _rmsnorm(x, gf_ref[...])
    o_ref[...] = x


def _ffn(x, g, wg, wu, wd, gf, *, final_norm, tm=512, ff_chunk=256):
    T, D = x.shape
    d_ff = wg.shape[1]
    kern = functools.partial(_ffn_kernel, ff_chunk=ff_chunk, final_norm=final_norm)
    return pl.pallas_call(
        kern,
        out_shape=jax.ShapeDtypeStruct((T, D), F32),
        grid=(T // tm,),
        in_specs=[
            pl.BlockSpec((tm, D), lambda i: (i, 0)),
            _const_spec((1, D)),
            _const_spec((D, d_ff)),
            _const_spec((D, d_ff)),
            _const_spec((d_ff, D)),
            _const_spec((1, D)),
        ],
        out_specs=pl.BlockSpec((tm, D), lambda i: (i, 0)),
        scratch_shapes=[pltpu.VMEM((tm, d_ff), BF16)],
        compiler_params=_params(1),
        name="ffn_final" if final_norm else "ffn",
    )(x, g, wg, wu, wd, gf)


def _inproj_kernel(x_ref, g_ref, w_ref, wt_ref, *o_refs, t_rows, t_scales, widths):
    hb = _rmsnorm(x_ref[...], g_ref[...]).astype(BF16)
    nt = (((1,), (1,)), ((), ()))
    off = 0
    for o_ref, rows, scale in zip(o_refs, t_rows, t_scales):
        for c in range(0, rows, 512):
            cw = min(512, rows - c)
            y = lax.dot_general(wt_ref[off + c:off + c + cw, :], hb, nt, preferred_element_type=F32)
            o_ref[c:c + cw, :] = (y * scale if scale != 1.0 else y).astype(BF16)
        off += rows
    off = 0
    for o_ref, width in zip(o_refs[len(t_rows):], widths):
        for c in range(0, width, 512):
            cw = min(512, width - c)
            y = jnp.dot(hb, w_ref[:, off + c:off + c + cw], preferred_element_type=F32)
            o_ref[:, c:c + cw] = y.astype(BF16)
        off += width


def _inproj(x, g, w, wt, t_rows, t_scales, widths, *, tm=512):
    T, D = x.shape
    kern = functools.partial(_inproj_kernel, t_rows=t_rows, t_scales=t_scales, widths=widths)
    return pl.pallas_call(
        kern,
        out_shape=[jax.ShapeDtypeStruct((rows, T), BF16) for rows in t_rows]
        + [jax.ShapeDtypeStruct((T, wd), BF16) for wd in widths],
        grid=(T // tm,),
        in_specs=[
            pl.BlockSpec((tm, D), lambda i: (i, 0)),
            _const_spec((1, D)),
            _const_spec(w.shape),
            _const_spec(wt.shape),
        ],
        out_specs=[pl.BlockSpec((rows, tm), lambda i: (0, i)) for rows in t_rows]
        + [pl.BlockSpec((tm, wd), lambda i: (i, 0)) for wd in widths],
        compiler_params=_params(1),
        name="in_proj",
    )(x, g, w, wt)


def _split3(x):
    def top_bits(y):
        return lax.bitcast_convert_type(lax.bitcast_convert_type(y, jnp.uint32) & jnp.uint32(0xFFFF0000), F32)
    hi = top_bits(x)
    mid = top_bits(x - hi)
    return [hi, mid, x - hi - mid]


def _da_kernel(slopes_ref, qt_ref, k_ref, vt_ref, dist_ref, lam_ref, sub_ref, o_ref,
               kaug_scr, vaug_scr, knorm_scr, *, lam_init):
    h = pl.program_id(1)
    step = pl.program_id(2)
    tq = DA_TQ
    S = k_ref.shape[0]
    nk = S // tq
    slope2 = slopes_ref[h] * LOG2E

    lp = lam_ref[...]
    lam = (jnp.exp(jnp.sum(lp[0:1] * lp[1:2], axis=-1, keepdims=True))
           - jnp.exp(jnp.sum(lp[2:3] * lp[3:4], axis=-1, keepdims=True)) + lam_init)

    @pl.when(step == 0)
    def _():
        k = k_ref[...]
        lane = lax.broadcasted_iota(jnp.int32, (S, LANES), 1)
        low = lane < HALF
        ones3 = jnp.where((lane % HALF) < 3, 1.0, 0.0).astype(BF16)
        kaug_scr[0] = jnp.where(low, k, ones3)
        kaug_scr[1] = jnp.where(low, ones3, k)
        kf = k.astype(F32)
        ksq = kf * kf
        for c in range(2):
            n2 = jnp.sum(jnp.where(low if c == 0 else ~low, ksq, 0.0), axis=1, keepdims=True)
            knorm_scr[c] = jnp.broadcast_to(jnp.max(n2, axis=0, keepdims=True), (1, LANES))
        vaug_scr[:LANES, :] = vt_ref[...]
        first = lax.broadcasted_iota(jnp.int32, (DA_VAUG_ROWS - LANES, S), 0) == 0
        vaug_scr[LANES:, :] = jnp.where(first, 1.0, 0.0).astype(BF16)

    row = lax.broadcasted_iota(jnp.int32, (HALF, tq), 0)

    def q_halves(sub):
        qt = qt_ref[:, sub * tq:(sub + 1) * tq]
        return qt[:HALF], qt[HALF:]

    def bias_tile(qi, J):
        return dist_ref[J - qi + (nk - 1)] * slope2

    def start(sub, stab):
        halves = q_halves(sub)
        qm = []
        for c in range(2):
            p = _split3(-stab[2 * sub + c])
            spare = jnp.where(row == 0, p[0], jnp.where(row == 1, p[1], jnp.where(row == 2, p[2], 0.0)))
            spare = spare.astype(BF16)
            qm.append(jnp.concatenate([halves[0], spare] if c == 0 else [spare, halves[1]], axis=0))
        return dict(sub=sub, qi=step * DA_SUBBLOCKS + sub, qm=qm, e={}, acc=[None, None])

    def score_piece(st, J):
        keys = slice(J * tq, (J + 1) * tq)
        bias = bias_tile(st["qi"], J)
        for c in range(2):
            s = jnp.dot(kaug_scr[c, keys, :], st["qm"][c], preferred_element_type=F32)
            st["e"][J, c] = jnp.exp2(s - bias).astype(BF16)

    def value_piece(st, J):
        keys = slice(J * tq, (J + 1) * tq)
        for c in range(2):
            pv = jnp.dot(vaug_scr[:, keys], st["e"].pop((J, c)), preferred_element_type=F32)
            st["acc"][c] = pv if J == 0 else st["acc"][c] + pv

    def finish(st):
        (o1, l1), (o2, l2) = ((a[:LANES], a[LANES:LANES + 1]) for a in st["acc"])
        ot = o1 * (1.0 / l1) - o2 * (lam / l2)
        o = _rmsnorm(ot.T, sub_ref[...]) * (1.0 - lam_init)
        o_ref[st["sub"] * tq:(st["sub"] + 1) * tq, :] = o.astype(BF16)

    def stream(stab):
        blocks = {}
        n_pieces = DA_SUBBLOCKS * nk
        for g in range(n_pieces + DA_LAG):
            if g < n_pieces:
                if g % nk == 0:
                    blocks[g // nk] = start(g // nk, stab)
                score_piece(blocks[g // nk], g % nk)
            v = g - DA_LAG
            if v >= 0:
                value_piece(blocks[v // nk], v % nk)
                if v % nk == nk - 1:
                    finish(blocks.pop(v // nk))

    bounds = []
    for sub in range(DA_SUBBLOCKS):
        for c, qh in enumerate(q_halves(sub)):
            qf = qh.astype(F32)
            qn2 = jnp.sum(qf * qf, axis=0, keepdims=True)
            bounds.append(jnp.sqrt(qn2 * knorm_scr[c][:, :1]) * 1.001 + 1e-3)
    stream(bounds)

    worst = bounds[0]
    for b in bounds[1:]:
        worst = jnp.maximum(worst, b)

    @pl.when(jnp.max(worst) > DA_MAX_BOUND)
    def _():
        k = k_ref[...]
        exact = []
        for sub in range(DA_SUBBLOCKS):
            qi = step * DA_SUBBLOCKS + sub
            bias = jnp.concatenate([bias_tile(qi, J) for J in range(nk)], axis=0)
            q1, q2 = q_halves(sub)
            zero = jnp.zeros_like(q1)
            for qm in (jnp.concatenate([q1, zero], axis=0), jnp.concatenate([zero, q2], axis=0)):
                s = jnp.dot(k, qm, preferred_element_type=F32) - bias
                exact.append(jnp.max(s, axis=0, keepdims=True))
        stream(exact)


def _da_attention(qt, k, vt, dist, da_lambda, subnorm, slopes, *, B, S, lam_init):
    T = k.shape[0]
    tq = DA_TQ * DA_SUBBLOCKS
    nq = S // tq
    kern = functools.partial(_da_kernel, lam_init=lam_init)
    return pl.pallas_call(
        kern,
        out_shape=jax.ShapeDtypeStruct((T, DA_HEADS * LANES), BF16),
        grid=(B, DA_HEADS, nq),
        in_specs=[
            pl.BlockSpec(memory_space=pltpu.SMEM),
            pl.BlockSpec((LANES, tq), lambda b, h, i: (h, b * nq + i)),
            pl.BlockSpec((S, LANES), lambda b, h, i: (b, h)),
            pl.BlockSpec((LANES, S), lambda b, h, i: (h, b)),
            _const_spec(dist.shape),
            _const_spec((4, HALF)),
            _const_spec((1, LANES)),
        ],
        out_specs=pl.BlockSpec((tq, LANES), lambda b, h, i: (b * nq + i, h)),
        scratch_shapes=[
            pltpu.VMEM((2, S, LANES), BF16),
            pltpu.VMEM((DA_VAUG_ROWS, S), BF16),
            pltpu.VMEM((2, 1, LANES), F32),
        ],
        compiler_params=pltpu.CompilerParams(
            dimension_semantics=("parallel", "parallel", "arbitrary"), vmem_limit_bytes=VMEM_LIMIT),
        name="da_attn",
    )(slopes, qt, k, vt, dist, da_lambda, subnorm)


def _da_distance_table(S):
    nk = S // DA_TQ
    t = jnp.arange(2 * nk - 1, dtype=jnp.int32)[:, None, None] - (nk - 1)
    r = jnp.arange(DA_TQ, dtype=jnp.int32)[None, :, None]
    m = jnp.arange(DA_TQ, dtype=jnp.int32)[None, None, :]
    return jnp.abs(t * DA_TQ + r - m).astype(F32)


def _sw_kernel(sink_ref, qt_ref, kk_ref, vt_ref, bias_ref, o_ref, kaug_scr, vaug_scr, knorm_scr):
    j = pl.program_id(1)
    ch = pl.program_id(2)
    S = kk_ref.shape[0]
    nb = S // Q_BLOCK
    wide = SW_GROUP * Q_BLOCK
    sink2 = jnp.concatenate(
        [jnp.full((1, Q_BLOCK), sink_ref[j * SW_GROUP + g] * LOG2E, F32) for g in range(SW_GROUP)], axis=1)

    @pl.when(ch == 0)
    def _():
        kk = kk_ref[...]
        lane = lax.broadcasted_iota(jnp.int32, (S, LANES), 1)
        low = lane < HALF
        kaug_scr[...] = jnp.where(low, kk, jnp.where(lane < HALF + 3, 1.0, 0.0).astype(BF16))
        kf = kk.astype(F32)
        n2 = jnp.sum(jnp.where(low, kf * kf, 0.0), axis=1, keepdims=True)
        knorm_scr[...] = jnp.broadcast_to(jnp.max(n2, axis=0, keepdims=True), (1, LANES))
        first = lax.broadcasted_iota(jnp.int32, (SW_VAUG_ROWS - HALF, Q_BLOCK), 0) == 0
        for n in range(nb):
            vaug_scr[n, :HALF, :] = vt_ref[:, n * Q_BLOCK:(n + 1) * Q_BLOCK]
            vaug_scr[n, HALF:, :] = jnp.where(first, 1.0, 0.0).astype(BF16)

    row = lax.broadcasted_iota(jnp.int32, (HALF, wide), 0)

    def block(r):
        n = ch * SW_BLOCKS_PER_STEP + r
        first_blk = jnp.clip(n - 1, 0, nb - 3)
        variant = jnp.where(n == 0, 0, jnp.where(n == nb - 1, 2, 1))
        qt4 = qt_ref[:, r * Q_BLOCK:(r + 1) * Q_BLOCK]
        q_wide = jnp.concatenate([qt4[g * HALF:(g + 1) * HALF] for g in range(SW_GROUP)], axis=1)
        return dict(r=r, first_blk=first_blk, variant=variant, q_wide=q_wide)

    def key_window(blk):
        return kaug_scr[pl.ds(pl.multiple_of(blk["first_blk"] * Q_BLOCK, Q_BLOCK), SW_WIN), :]

    blocks = [block(r) for r in range(SW_BLOCKS_PER_STEP)]

    def scores(blk, m):
        p = _split3(-m)
        spare = jnp.where(row == 0, p[0], jnp.where(row == 1, p[1], jnp.where(row == 2, p[2], 0.0))).astype(BF16)
        qm = jnp.concatenate([blk["q_wide"], spare], axis=0)
        s = jnp.dot(key_window(blk), qm, preferred_element_type=F32)
        return jnp.exp2(s - bias_ref[blk["variant"]]).astype(BF16)

    def values(blk, m, e):
        fb = blk["first_blk"]
        vwin = jnp.concatenate([vaug_scr[fb], vaug_scr[fb + 1], vaug_scr[fb + 2]], axis=1)
        ov = jnp.dot(vwin, e, preferred_element_type=F32)
        denom = ov[HALF:HALF + 1] + jnp.exp2(sink2 - m)
        on = ov[:HALF] * (1.0 / denom)
        rows = slice(blk["r"] * Q_BLOCK, (blk["r"] + 1) * Q_BLOCK)
        for t in range(SW_GROUP // 2):
            pair = jnp.concatenate([on[:, (2 * t) * Q_BLOCK:(2 * t + 1) * Q_BLOCK],
                                    on[:, (2 * t + 1) * Q_BLOCK:(2 * t + 2) * Q_BLOCK]], axis=0)
            o_ref[rows, t * LANES:(t + 1) * LANES] = pair.T.astype(BF16)

    def stream(stabilisers):
        e = {}
        for r in range(SW_BLOCKS_PER_STEP + 1):
            if r < SW_BLOCKS_PER_STEP:
                e[r] = scores(blocks[r], stabilisers[r])
            if r >= 1:
                values(blocks[r - 1], stabilisers[r - 1], e.pop(r - 1))

    bounds = []
    for blk in blocks:
        qf = blk["q_wide"].astype(F32)
        qn2 = jnp.sum(qf * qf, axis=0, keepdims=True)
        bounds.append(jnp.sqrt(qn2 * knorm_scr[:, :1]) * 1.001 + 1e-3)
    stream([jnp.maximum(b, sink2) for b in bounds])

    worst = bounds[0]
    for b in bounds[1:]:
        worst = jnp.maximum(worst, b)

    @pl.when(jnp.max(worst) > DA_MAX_BOUND)
    def _():
        exact = []
        for blk in blocks:
            qm = jnp.concatenate([blk["q_wide"], jnp.zeros_like(blk["q_wide"])], axis=0)
            s = jnp.dot(key_window(blk), qm, preferred_element_type=F32) - bias_ref[blk["variant"]]
            exact.append(jnp.maximum(jnp.max(s, axis=0, keepdims=True), sink2))
        stream(exact)


def _sw_attention(qt, kk, vt, bias, sink, *, B, S):
    T = kk.shape[0]
    tq = SW_BLOCKS_PER_STEP * Q_BLOCK
    nq = S // tq
    nb = S // Q_BLOCK
    gw = SW_GROUP * HALF
    return pl.pallas_call(
        _sw_kernel,
        out_shape=jax.ShapeDtypeStruct((T, SW_Q_HEADS * HALF), BF16),
        grid=(B, SW_KV_HEADS, nq),
        in_specs=[
            pl.BlockSpec(memory_space=pltpu.SMEM),
            pl.BlockSpec((gw, tq), lambda b, j, n: (j, b * nq + n)),
            pl.BlockSpec((S, LANES), lambda b, j, n: (b, j)),
            pl.BlockSpec((HALF, S), lambda b, j, n: (j, b)),
            pl.BlockSpec((None,) + bias.shape[1:], lambda b, j, n: (j, 0, 0, 0)),
        ],
        out_specs=pl.BlockSpec((tq, gw), lambda b, j, n: (b * nq + n, j)),
        scratch_shapes=[
            pltpu.VMEM((S, LANES), BF16),
            pltpu.VMEM((nb, SW_VAUG_ROWS, Q_BLOCK), BF16),
            pltpu.VMEM((1, LANES), F32),
        ],
        compiler_params=pltpu.CompilerParams(
            dimension_semantics=("parallel", "parallel", "arbitrary"), vmem_limit_bytes=VMEM_LIMIT),
        name="sw_attn",
    )(sink, qt, kk, vt, bias)


def _sw_bias_table(slopes):
    shift = jnp.arange(3, dtype=jnp.int32)[:, None, None] * Q_BLOCK
    c = jnp.arange(SW_WIN, dtype=jnp.int32)[None, :, None]
    r = jnp.arange(Q_BLOCK, dtype=jnp.int32)[None, None, :]
    d = jnp.abs(shift + r - c)
    dist = jnp.where(d <= WINDOW, d.astype(F32), MASKED_DIST)
    s2 = (slopes * LOG2E).reshape(SW_KV_HEADS, 1, 1, SW_GROUP, 1)
    table = jnp.minimum(dist[None, :, :, None, :] * s2, MASKED_DIST)
    return table.reshape(SW_KV_HEADS, 3, SW_WIN, SW_GROUP * Q_BLOCK)


def _merge_kernel(x_ref, g_ref, wgate_ref, bg_ref, oda_ref, osw_ref, pa_ref, pb_ref, wo_ref, o_ref):
    x = x_ref[...]
    D = x.shape[1]
    hb = _rmsnorm(x, g_ref[...]).astype(BF16)
    gates = jnp.dot(hb, wgate_ref[...], preferred_element_type=F32) + bg_ref[...]
    gates = jax.nn.sigmoid(gates)
    ya = jnp.dot(oda_ref[...], pa_ref[...], preferred_element_type=F32)
    yb = jnp.dot(osw_ref[...], pb_ref[...], preferred_element_type=F32)
    merged = gates[:, :D] * ya + gates[:, D:] * yb
    o_ref[...] = x + jnp.dot(merged.astype(BF16), wo_ref[...], preferred_element_type=F32)


def _merge(x, g, wgate, bg, oda, osw, pa, pb, wo, *, tm=512):
    T, D = x.shape
    row = lambda i: (i, 0)
    return pl.pallas_call(
        _merge_kernel,
        out_shape=jax.ShapeDtypeStruct((T, D), F32),
        grid=(T // tm,),
        in_specs=[
            pl.BlockSpec((tm, D), row),
            _const_spec((1, D)),
            _const_spec((D, 2 * D)),
            _const_spec((1, 2 * D)),
            pl.BlockSpec((tm, D), row),
            pl.BlockSpec((tm, D), row),
            _const_spec((D, D)),
            _const_spec((D, D)),
            _const_spec((D, D)),
        ],
        out_specs=pl.BlockSpec((tm, D), row),
        compiler_params=_params(1),
        name="merge",
    )(x, g, wgate, bg, oda, osw, pa, pb, wo)


def _alibi_slopes(n):
    return 2.0 ** (-8.0 * jnp.arange(1, n + 1, dtype=F32) / n)


def kernel(x, norm_ffn1, ffn1_gate, ffn1_up, ffn1_down, norm_mix, w_in, b_gate, da_lambda, da_subnorm, swa_sink, w_proj_da, w_proj_swa, w_out, norm_ffn2, ffn2_gate, ffn2_up, ffn2_down, norm_final):
    B, S, D = x.shape
    depth = norm_ffn1.shape[0]
    assert depth >= 1, "the final RMSNorm is fused into the last layer's second FFN"
    T = B * S
    xt = x.reshape(T, D)
    bf = lambda a: a.astype(BF16)
    row = lambda a: a.reshape(1, -1)
    gf = row(norm_final)

    da_w = DA_HEADS * LANES
    swq_w = SW_Q_HEADS * HALF
    swkv_w = SW_KV_HEADS * HALF
    o_k, o_v = da_w, 2 * da_w
    o_sq = o_v + da_w
    o_sk = o_sq + swq_w
    o_sv = o_sk + swkv_w
    o_gate = o_sv + swkv_w

    def dup_heads(w):
        w = w.reshape(D, SW_KV_HEADS, 1, HALF)
        return jnp.broadcast_to(w, (D, SW_KV_HEADS, 2, HALF)).reshape(D, SW_KV_HEADS * LANES)

    t_rows = (da_w, da_w, swq_w, swkv_w)
    qscale = HALF ** -0.5 * LOG2E
    t_scales = (qscale, 1.0, qscale, 1.0)
    widths = (da_w, SW_KV_HEADS * LANES)
    da_dist = _da_distance_table(S)
    sw_bias = _sw_bias_table(_alibi_slopes(SW_Q_HEADS))

    for l in range(depth):
        lam_init = 0.8 - 0.6 * math.exp(-0.3 * l)
        wl = w_in[l]
        w_cols = bf(jnp.concatenate([wl[:, o_k:o_v], dup_heads(wl[:, o_sk:o_sv])], axis=1))
        w_t = bf(jnp.concatenate([wl[:, :o_k], wl[:, o_v:o_sk], wl[:, o_sv:o_gate]], axis=1).T)

        xt = _ffn(xt, row(norm_ffn1[l]), bf(ffn1_gate[l]), bf(ffn1_up[l]), bf(ffn1_down[l]), gf,
                  final_norm=False)
        da_qt, da_vt, sw_qt, sw_vt, da_k, sw_kk = _inproj(
            xt, row(norm_mix[l]), w_cols, w_t, t_rows, t_scales, widths)
        o_da = _da_attention(da_qt, da_k, da_vt, da_dist, da_lambda[l], row(da_subnorm[l]),
                             _alibi_slopes(DA_HEADS), B=B, S=S, lam_init=lam_init)
        o_sw = _sw_attention(sw_qt, sw_kk, sw_vt, sw_bias, swa_sink[l], B=B, S=S)
        xt = _merge(xt, row(norm_mix[l]), bf(wl[:, o_gate:]), row(b_gate[l]), o_da, o_sw,
                    bf(w_proj_da[l]), bf(w_proj_swa[l]), bf(w_out[l]))
        xt = _ffn(xt, row(norm_ffn2[l]), bf(ffn2_gate[l]), bf(ffn2_up[l]), bf(ffn2_down[l]), gf,
                  final_norm=(l == depth - 1))
    return xt.reshape(B, S, D)
```

```python
import functools
import math

import jax
import jax.numpy as jnp
from jax import lax
from jax.experimental import pallas as pl
from jax.experimental.pallas import tpu as pltpu

F32 = jnp.float32
BF16 = jnp.bfloat16

RMS_EPS = 1e-6
LOG2E = math.log2(math.e)
LANES = 128
HALF = 64
VMEM_LIMIT = 56 * 1024 * 1024
MASKED_DIST = 1e30

DA_HEADS = 8
DA_TQ = 256
DA_SUBBLOCKS = 8
DA_VAUG_ROWS = LANES + 16
DA_LAG = 2
DA_MAX_BOUND = 40.0
SW_Q_HEADS = 16
SW_KV_HEADS = 4
SW_GROUP = SW_Q_HEADS // SW_KV_HEADS
SW_BLOCKS_PER_STEP = 16
WINDOW = 128
Q_BLOCK = 128
SW_WIN = 3 * Q_BLOCK
SW_VAUG_ROWS = HALF + 16


def _rmsnorm(x, g):
    return x * lax.rsqrt(jnp.mean(x * x, axis=-1, keepdims=True) + RMS_EPS) * g


def _const_spec(shape):
    return pl.BlockSpec(shape, lambda *_: (0,) * len(shape), pipeline_mode=pl.Buffered(1))


def _params(n_axes):
    return pltpu.CompilerParams(
        dimension_semantics=("parallel",) * n_axes, vmem_limit_bytes=VMEM_LIMIT)


def _ffn_kernel(x_ref, g_ref, wg_ref, wu_ref, wd_ref, gf_ref, o_ref, a_scr, *, ff_chunk, final_norm):
    x = x_ref[...]
    hb = _rmsnorm(x, g_ref[...]).astype(BF16)
    d_ff = wg_ref.shape[1]
    for c in range(d_ff // ff_chunk):
        sl = slice(c * ff_chunk, (c + 1) * ff_chunk)
        gate = jnp.dot(hb, wg_ref[:, sl], preferred_element_type=F32)
        up = jnp.dot(hb, wu_ref[:, sl], preferred_element_type=F32)
        a_scr[:, sl] = (gate * jax.nn.sigmoid(gate) * up).astype(BF16)
    y = jnp.dot(a_scr[...], wd_ref[...], preferred_element_type=F32)
    x = x + 0.5 * y
    if final_norm:
        x = _rmsnorm(x, gf_ref[...])
    o_ref[...] = x


def _ffn(x, g, wg, wu, wd, gf, *, final_norm, tm=1024, ff_chunk=256):
    T, D = x.shape
    d_ff = wg.shape[1]
    kern = functools.partial(_ffn_kernel, ff_chunk=ff_chunk, final_norm=final_norm)
    return pl.pallas_call(
        kern,
        out_shape=jax.ShapeDtypeStruct((T, D), F32),
        grid=(T // tm,),
        in_specs=[
            pl.BlockSpec((tm, D), lambda i: (i, 0)),
            _const_spec((1, D)),
            _const_spec((D, d_ff)),
            _const_spec((D, d_ff)),
            _const_spec((d_ff, D)),
            _const_spec((1, D)),
        ],
        out_specs=pl.BlockSpec((tm, D), lambda i: (i, 0)),
        scratch_shapes=[pltpu.VMEM((tm, d_ff), BF16)],
        compiler_params=_params(1),
        name="ffn_final" if final_norm else "ffn",
    )(x, g, wg, wu, wd, gf)


def _inproj_kernel(x_ref, g_ref, w_ref, wt_ref, *o_refs, t_rows, t_scales, widths):
    hb = _rmsnorm(x_ref[...], g_ref[...]).astype(BF16)
    nt = (((1,), (1,)), ((), ()))
    off = 0
    for o_ref, rows, scale in zip(o_refs, t_rows, t_scales):
        for c in range(0, rows, 512):
            cw = min(512, rows - c)
            y = lax.dot_general(wt_ref[off + c:off + c + cw, :], hb, nt, preferred_element_type=F32)
            o_ref[c:c + cw, :] = (y * scale if scale != 1.0 else y).astype(BF16)
        off += rows
    off = 0
    for o_ref, width in zip(o_refs[len(t_rows):], widths):
        for c in range(0, width, 512):
            cw = min(512, width - c)
            y = jnp.dot(hb, w_ref[:, off + c:off + c + cw], preferred_element_type=F32)
            o_ref[:, c:c + cw] = y.astype(BF16)
        off += width


def _inproj(x, g, w, wt, t_rows, t_scales, widths, *, tm=512):
    T, D = x.shape
    kern = functools.partial(_inproj_kernel, t_rows=t_rows, t_scales=t_scales, widths=widths)
    return pl.pallas_call(
        kern,
        out_shape=[jax.ShapeDtypeStruct((rows, T), BF16) for rows in t_rows]
        + [jax.ShapeDtypeStruct((T, wd), BF16) for wd in widths],
        grid=(T // tm,),
        in_specs=[
            pl.BlockSpec((tm, D), lambda i: (i, 0)),
            _const_spec((1, D)),
            _const_spec(w.shape),
            _const_spec(wt.shape),
        ],
        out_specs=[pl.BlockSpec((rows, tm), lambda i: (0, i)) for rows in t_rows]
        + [pl.BlockSpec((tm, wd), lambda i: (i, 0)) for wd in widths],
        compiler_params=_params(1),
        name="in_proj",
    )(x, g, w, wt)


def _split3(x):
    def top_bits(y):
        return lax.bitcast_convert_type(lax.bitcast_convert_type(y, jnp.uint32) & jnp.uint32(0xFFFF0000), F32)
    hi = top_bits(x)
    mid = top_bits(x - hi)
    return [hi, mid, x - hi - mid]


def _da_kernel(slopes_ref, qt_ref, k_ref, vt_ref, dist_ref, lam_ref, sub_ref, o_ref,
               kaug_scr, vaug_scr, knorm_scr, *, lam_init):
    h = pl.program_id(1)
    step = pl.program_id(2)
    tq = DA_TQ
    S = k_ref.shape[0]
    nk = S // tq
    slope2 = slopes_ref[h] * LOG2E

    lp = lam_ref[...]
    lam = (jnp.exp(jnp.sum(lp[0:1] * lp[1:2], axis=-1, keepdims=True))
           - jnp.exp(jnp.sum(lp[2:3] * lp[3:4], axis=-1, keepdims=True)) + lam_init)

    @pl.when(step == 0)
    def _():
        k = k_ref[...]
        lane = lax.broadcasted_iota(jnp.int32, (S, LANES), 1)
        low = lane < HALF
        ones3 = jnp.where((lane % HALF) < 3, 1.0, 0.0).astype(BF16)
        kaug_scr[0] = jnp.where(low, k, ones3)
        kaug_scr[1] = jnp.where(low, ones3, k)
        kf = k.astype(F32)
        ksq = kf * kf
        for c in range(2):
            n2 = jnp.sum(jnp.where(low if c == 0 else ~low, ksq, 0.0), axis=1, keepdims=True)
            knorm_scr[c] = jnp.broadcast_to(jnp.max(n2, axis=0, keepdims=True), (1, LANES))
        vaug_scr[:LANES, :] = vt_ref[...]
        first = lax.broadcasted_iota(jnp.int32, (DA_VAUG_ROWS - LANES, S), 0) == 0
        vaug_scr[LANES:, :] = jnp.where(first, 1.0, 0.0).astype(BF16)

    row = lax.broadcasted_iota(jnp.int32, (HALF, tq), 0)

    def q_halves(sub):
        qt = qt_ref[:, sub * tq:(sub + 1) * tq]
        return qt[:HALF], qt[HALF:]

    def bias_tile(qi, J):
        return dist_ref[J - qi + (nk - 1)] * slope2

    def start(sub, stab):
        halves = q_halves(sub)
        qm = []
        for c in range(2):
            p = _split3(-stab[2 * sub + c])
            spare = jnp.where(row == 0, p[0], jnp.where(row == 1, p[1], jnp.where(row == 2, p[2], 0.0)))
            spare = spare.astype(BF16)
            qm.append(jnp.concatenate([halves[0], spare] if c == 0 else [spare, halves[1]], axis=0))
        return dict(sub=sub, qi=step * DA_SUBBLOCKS + sub, qm=qm, e={}, acc=[None, None])

    def score_piece(st, J):
        keys = slice(J * tq, (J + 1) * tq)
        bias = bias_tile(st["qi"], J)
        for c in range(2):
            s = jnp.dot(kaug_scr[c, keys, :], st["qm"][c], preferred_element_type=F32)
            st["e"][J, c] = jnp.exp2(s - bias).astype(BF16)

    def value_piece(st, J):
        keys = slice(J * tq, (J + 1) * tq)
        for c in range(2):
            pv = jnp.dot(vaug_scr[:, keys], st["e"].pop((J, c)), preferred_element_type=F32)
            st["acc"][c] = pv if J == 0 else st["acc"][c] + pv

    def finish(st):
        (o1, l1), (o2, l2) = ((a[:LANES], a[LANES:LANES + 1]) for a in st["acc"])
        ot = o1 * (1.0 / l1) - o2 * (lam / l2)
        o = _rmsnorm(ot.T, sub_ref[...]) * (1.0 - lam_init)
        o_ref[st["sub"] * tq:(st["sub"] + 1) * tq, :] = o.astype(BF16)

    def stream(stab):
        blocks = {}
        n_pieces = DA_SUBBLOCKS * nk
        for g in range(n_pieces + DA_LAG):
            if g < n_pieces:
                if g % nk == 0:
                    blocks[g // nk] = start(g // nk, stab)
                score_piece(blocks[g // nk], g % nk)
            v = g - DA_LAG
            if v >= 0:
                value_piece(blocks[v // nk], v % nk)
                if v % nk == nk - 1:
                    finish(blocks.pop(v // nk))

    bounds = []
    for sub in range(DA_SUBBLOCKS):
        for c, qh in enumerate(q_halves(sub)):
            qf = qh.astype(F32)
            qn2 = jnp.sum(qf * qf, axis=0, keepdims=True)
            bounds.append(jnp.sqrt(qn2 * knorm_scr[c][:, :1]) * 1.001 + 1e-3)
    stream(bounds)

    worst = bounds[0]
    for b in bounds[1:]:
        worst = jnp.maximum(worst, b)

    @pl.when(jnp.max(worst) > DA_MAX_BOUND)
    def _():
        k = k_ref[...]
        exact = []
        for sub in range(DA_SUBBLOCKS):
            qi = step * DA_SUBBLOCKS + sub
            bias = jnp.concatenate([bias_tile(qi, J) for J in range(nk)], axis=0)
            q1, q2 = q_halves(sub)
            zero = jnp.zeros_like(q1)
            for qm in (jnp.concatenate([q1, zero], axis=0), jnp.concatenate([zero, q2], axis=0)):
                s = jnp.dot(k, qm, preferred_element_type=F32) - bias
                exact.append(jnp.max(s, axis=0, keepdims=True))
        stream(exact)


def _da_attention(qt, k, vt, dist, da_lambda, subnorm, slopes, *, B, S, lam_init):
    T = k.shape[0]
    tq = DA_TQ * DA_SUBBLOCKS
    nq = S // tq
    kern = functools.partial(_da_kernel, lam_init=lam_init)
    return pl.pallas_call(
        kern,
        out_shape=jax.ShapeDtypeStruct((T, DA_HEADS * LANES), BF16),
        grid=(B, DA_HEADS, nq),
        in_specs=[
            pl.BlockSpec(memory_space=pltpu.SMEM),
            pl.BlockSpec((LANES, tq), lambda b, h, i: (h, b * nq + i)),
            pl.BlockSpec((S, LANES), lambda b, h, i: (b, h)),
            pl.BlockSpec((LANES, S), lambda b, h, i: (h, b)),
            _const_spec(dist.shape),
            _const_spec((4, HALF)),
            _const_spec((1, LANES)),
        ],
        out_specs=pl.BlockSpec((tq, LANES), lambda b, h, i: (b * nq + i, h)),
        scratch_shapes=[
            pltpu.VMEM((2, S, LANES), BF16),
            pltpu.VMEM((DA_VAUG_ROWS, S), BF16),
            pltpu.VMEM((2, 1, LANES), F32),
        ],
        compiler_params=pltpu.CompilerParams(
            dimension_semantics=("parallel", "parallel", "arbitrary"), vmem_limit_bytes=VMEM_LIMIT),
        name="da_attn",
    )(slopes, qt, k, vt, dist, da_lambda, subnorm)


def _da_distance_table(S):
    nk = S // DA_TQ
    t = jnp.arange(2 * nk - 1, dtype=jnp.int32)[:, None, None] - (nk - 1)
    r = jnp.arange(DA_TQ, dtype=jnp.int32)[None, :, None]
    m = jnp.arange(DA_TQ, dtype=jnp.int32)[None, None, :]
    return jnp.abs(t * DA_TQ + r - m).astype(F32)


def _sw_kernel(sink_ref, qt_ref, kk_ref, vt_ref, bias_ref, o_ref, kaug_scr, vaug_scr, knorm_scr):
    j = pl.program_id(1)
    ch = pl.program_id(2)
    S = kk_ref.shape[0]
    nb = S // Q_BLOCK
    wide = SW_GROUP * Q_BLOCK
    sink2 = jnp.concatenate(
        [jnp.full((1, Q_BLOCK), sink_ref[j * SW_GROUP + g] * LOG2E, F32) for g in range(SW_GROUP)], axis=1)

    @pl.when(ch == 0)
    def _():
        kk = kk_ref[...]
        lane = lax.broadcasted_iota(jnp.int32, (S, LANES), 1)
        low = lane < HALF
        kaug_scr[...] = jnp.where(low, kk, jnp.where(lane < HALF + 3, 1.0, 0.0).astype(BF16))
        kf = kk.astype(F32)
        n2 = jnp.sum(jnp.where(low, kf * kf, 0.0), axis=1, keepdims=True)
        knorm_scr[...] = jnp.broadcast_to(jnp.max(n2, axis=0, keepdims=True), (1, LANES))
        first = lax.broadcasted_iota(jnp.int32, (SW_VAUG_ROWS - HALF, Q_BLOCK), 0) == 0
        for n in range(nb):
            vaug_scr[n, :HALF, :] = vt_ref[:, n * Q_BLOCK:(n + 1) * Q_BLOCK]
            vaug_scr[n, HALF:, :] = jnp.where(first, 1.0, 0.0).astype(BF16)

    row = lax.broadcasted_iota(jnp.int32, (HALF, wide), 0)

    def block(r):
        n = ch * SW_BLOCKS_PER_STEP + r
        first_blk = jnp.clip(n - 1, 0, nb - 3)
        variant = jnp.where(n == 0, 0, jnp.where(n == nb - 1, 2, 1))
        qt4 = qt_ref[:, r * Q_BLOCK:(r + 1) * Q_BLOCK]
        q_wide = jnp.concatenate([qt4[g * HALF:(g + 1) * HALF] for g in range(SW_GROUP)], axis=1)
        return dict(r=r, first_blk=first_blk, variant=variant, q_wide=q_wide)

    def key_window(blk):
        return kaug_scr[pl.ds(pl.multiple_of(blk["first_blk"] * Q_BLOCK, Q_BLOCK), SW_WIN), :]

    blocks = [block(r) for r in range(SW_BLOCKS_PER_STEP)]

    def scores(blk, m):
        p = _split3(-m)
        spare = jnp.where(row == 0, p[0], jnp.where(row == 1, p[1], jnp.where(row == 2, p[2], 0.0))).astype(BF16)
        qm = jnp.concatenate([blk["q_wide"], spare], axis=0)
        s = jnp.dot(key_window(blk), qm, preferred_element_type=F32)
        return jnp.exp2(s - bias_ref[blk["variant"]]).astype(BF16)

    def values(blk, m, e):
        fb = blk["first_blk"]
        vwin = jnp.concatenate([vaug_scr[fb], vaug_scr[fb + 1], vaug_scr[fb + 2]], axis=1)
        ov = jnp.dot(vwin, e, preferred_element_type=F32)
        denom = ov[HALF:HALF + 1] + jnp.exp2(sink2 - m)
        on = ov[:HALF] * (1.0 / denom)
        rows = slice(blk["r"] * Q_BLOCK, (blk["r"] + 1) * Q_BLOCK)
        for t in range(SW_GROUP // 2):
            pair = jnp.concatenate([on[:, (2 * t) * Q_BLOCK:(2 * t + 1) * Q_BLOCK],
                                    on[:, (2 * t + 1) * Q_BLOCK:(2 * t + 2) * Q_BLOCK]], axis=0)
            o_ref[rows, t * LANES:(t + 1) * LANES] = pair.T.astype(BF16)

    def stream(stabilisers):
        e = {}
        for r in range(SW_BLOCKS_PER_STEP + 1):
            if r < SW_BLOCKS_PER_STEP:
                e[r] = scores(blocks[r], stabilisers[r])
            if r >= 1:
                values(blocks[r - 1], stabilisers[r - 1], e.pop(r - 1))

    bounds = []
    for blk in blocks:
        qf = blk["q_wide"].astype(F32)
        qn2 = jnp.sum(qf * qf, axis=0, keepdims=True)
        bounds.append(jnp.sqrt(qn2 * knorm_scr[:, :1]) * 1.001 + 1e-3)
    stream([jnp.maximum(b, sink2) for b in bounds])

    worst = bounds[0]
    for b in bounds[1:]:
        worst = jnp.maximum(worst, b)

    @pl.when(jnp.max(worst) > DA_MAX_BOUND)
    def _():
        exact = []
        for blk in blocks:
            qm = jnp.concatenate([blk["q_wide"], jnp.zeros_like(blk["q_wide"])], axis=0)
            s = jnp.dot(key_window(blk), qm, preferred_element_type=F32) - bias_ref[blk["variant"]]
            exact.append(jnp.maximum(jnp.max(s, axis=0, keepdims=True), sink2))
        stream(exact)


def _sw_attention(qt, kk, vt, bias, sink, *, B, S):
    T = kk.shape[0]
    tq = SW_BLOCKS_PER_STEP * Q_BLOCK
    nq = S // tq
    nb = S // Q_BLOCK
    gw = SW_GROUP * HALF
    return pl.pallas_call(
        _sw_kernel,
        out_shape=jax.ShapeDtypeStruct((T, SW_Q_HEADS * HALF), BF16),
        grid=(B, SW_KV_HEADS, nq),
        in_specs=[
            pl.BlockSpec(memory_space=pltpu.SMEM),
            pl.BlockSpec((gw, tq), lambda b, j, n: (j, b * nq + n)),
            pl.BlockSpec((S, LANES), lambda b, j, n: (b, j)),
            pl.BlockSpec((HALF, S), lambda b, j, n: (j, b)),
            pl.BlockSpec((None,) + bias.shape[1:], lambda b, j, n: (j, 0, 0, 0)),
        ],
        out_specs=pl.BlockSpec((tq, gw), lambda b, j, n: (b * nq + n, j)),
        scratch_shapes=[
            pltpu.VMEM((S, LANES), BF16),
            pltpu.VMEM((nb, SW_VAUG_ROWS, Q_BLOCK), BF16),
            pltpu.VMEM((1, LANES), F32),
        ],
        compiler_params=pltpu.CompilerParams(
            dimension_semantics=("parallel", "parallel", "arbitrary"), vmem_limit_bytes=VMEM_LIMIT),
        name="sw_attn",
    )(sink, qt, kk, vt, bias)


def _sw_bias_table(slopes):
    shift = jnp.arange(3, dtype=jnp.int32)[:, None, None] * Q_BLOCK
    c = jnp.arange(SW_WIN, dtype=jnp.int32)[None, :, None]
    r = jnp.arange(Q_BLOCK, dtype=jnp.int32)[None, None, :]
    d = jnp.abs(shift + r - c)
    dist = jnp.where(d <= WINDOW, d.astype(F32), MASKED_DIST)
    s2 = (slopes * LOG2E).reshape(SW_KV_HEADS, 1, 1, SW_GROUP, 1)
    table = jnp.minimum(dist[None, :, :, None, :] * s2, MASKED_DIST)
    return table.reshape(SW_KV_HEADS, 3, SW_WIN, SW_GROUP * Q_BLOCK)


def _merge_kernel(x_ref, g_ref, wgate_ref, bg_ref, oda_ref, osw_ref, pa_ref, pb_ref, wo_ref, o_ref):
    x = x_ref[...]
    D = x.shape[1]
    hb = _rmsnorm(x, g_ref[...]).astype(BF16)
    gates = jnp.dot(hb, wgate_ref[...], preferred_element_type=F32) + bg_ref[...]
    gates = jax.nn.sigmoid(gates)
    ya = jnp.dot(oda_ref[...], pa_ref[...], preferred_element_type=F32)
    yb = jnp.dot(osw_ref[...], pb_ref[...], preferred_element_type=F32)
    merged = gates[:, :D] * ya + gates[:, D:] * yb
    o_ref[...] = x + jnp.dot(merged.astype(BF16), wo_ref[...], preferred_element_type=F32)


def _merge(x, g, wgate, bg, oda, osw, pa, pb, wo, *, tm=512):
    T, D = x.shape
    row = lambda i: (i, 0)
    return pl.pallas_call(
        _merge_kernel,
        out_shape=jax.ShapeDtypeStruct((T, D), F32),
        grid=(T // tm,),
        in_specs=[
            pl.BlockSpec((tm, D), row),
            _const_spec((1, D)),
            _const_spec((D, 2 * D)),
            _const_spec((1, 2 * D)),
            pl.BlockSpec((tm, D), row),
            pl.BlockSpec((tm, D), row),
            _const_spec((D, D)),
            _const_spec((D, D)),
            _const_spec((D, D)),
        ],
        out_specs=pl.BlockSpec((tm, D), row),
        compiler_params=_params(1),
        name="merge",
    )(x, g, wgate, bg, oda, osw, pa, pb, wo)


def _alibi_slopes(n):
    return 2.0 ** (-8.0 * jnp.arange(1, n + 1, dtype=F32) / n)


def kernel(x, norm_ffn1, ffn1_gate, ffn1_up, ffn1_down, norm_mix, w_in, b_gate, da_lambda, da_subnorm, swa_sink, w_proj_da, w_proj_swa, w_out, norm_ffn2, ffn2_gate, ffn2_up, ffn2_down, norm_final):
    B, S, D = x.shape
    depth = norm_ffn1.shape[0]
    assert depth >= 1, "the final RMSNorm is fused into the last layer's second FFN"
    T = B * S
    xt = x.reshape(T, D)
    bf = lambda a: a.astype(BF16)
    row = lambda a: a.reshape(1, -1)
    gf = row(norm_final)

    da_w = DA_HEADS * LANES
    swq_w = SW_Q_HEADS * HALF
    swkv_w = SW_KV_HEADS * HALF
    o_k, o_v = da_w, 2 * da_w
    o_sq = o_v + da_w
    o_sk = o_sq + swq_w
    o_sv = o_sk + swkv_w
    o_gate = o_sv + swkv_w

    def dup_heads(w):
        w = w.reshape(D, SW_KV_HEADS, 1, HALF)
        return jnp.broadcast_to(w, (D, SW_KV_HEADS, 2, HALF)).reshape(D, SW_KV_HEADS * LANES)

    t_rows = (da_w, da_w, swq_w, swkv_w)
    qscale = HALF ** -0.5 * LOG2E
    t_scales = (qscale, 1.0, qscale, 1.0)
    widths = (da_w, SW_KV_HEADS * LANES)
    da_dist = _da_distance_table(S)
    sw_bias = _sw_bias_table(_alibi_slopes(SW_Q_HEADS))

    for l in range(depth):
        lam_init = 0.8 - 0.6 * math.exp(-0.3 * l)
        wl = w_in[l]
        w_cols = bf(jnp.concatenate([wl[:, o_k:o_v], dup_heads(wl[:, o_sk:o_sv])], axis=1))
        w_t = bf(jnp.concatenate([wl[:, :o_k], wl[:, o_v:o_sk], wl[:, o_sv:o_gate]], axis=1).T)

        xt = _ffn(xt, row(norm_ffn1[l]), bf(ffn1_gate[l]), bf(ffn1_up[l]), bf(ffn1_down[l]), gf,
                  final_norm=False)
        da_qt, da_vt, sw_qt, sw_vt, da_k, sw_kk = _inproj(
            xt, row(norm_mix[l]), w_cols, w_t, t_rows, t_scales, widths)
        o_da = _da_attention(da_qt, da_k, da_vt, da_dist, da_lambda[l], row(da_subnorm[l]),
                             _alibi_slopes(DA_HEADS), B=B, S=S, lam_init=lam_init)
        o_sw = _sw_attention(sw_qt, sw_kk, sw_vt, sw_bias, swa_sink[l], B=B, S=S)
        xt = _merge(xt, row(norm_mix[l]), bf(wl[:, o_gate:]), row(b_gate[l]), o_da, o_sw,
                    bf(w_proj_da[l]), bf(w_proj_swa[l]), bf(w_out[l]))
        xt = _ffn(xt, row(norm_ffn2[l]), bf(ffn2_gate[l]), bf(ffn2_up[l]), bf(ffn2_down[l]), gf,
                  final_norm=(l == depth - 1))
    return xt.reshape(B, S, D)
```

```python
import functools
import math

import jax
import jax.numpy as jnp
from jax import lax
from jax.experimental import pallas as pl
from jax.experimental.pallas import tpu as pltpu

F32 = jnp.float32
BF16 = jnp.bfloat16

RMS_EPS = 1e-6
LOG2E = math.log2(math.e)
LANES = 128
HALF = 64
VMEM_LIMIT = 56 * 1024 * 1024
MASKED_DIST = 1e30

DA_HEADS = 8
DA_TQ = 256
DA_SUBBLOCKS = 8
DA_VAUG_ROWS = LANES + 16
DA_LAG = 2
DA_MAX_BOUND = 40.0
SW_Q_HEADS = 16
SW_KV_HEADS = 4
SW_GROUP = SW_Q_HEADS // SW_KV_HEADS
SW_BLOCKS_PER_STEP = 16
WINDOW = 128
Q_BLOCK = 128
SW_WIN = 3 * Q_BLOCK
SW_VAUG_ROWS = HALF + 16


def _rmsnorm(x, g):
    return x * lax.rsqrt(jnp.mean(x * x, axis=-1, keepdims=True) + RMS_EPS) * g


def _const_spec(shape):
    return pl.BlockSpec(shape, lambda *_: (0,) * len(shape), pipeline_mode=pl.Buffered(1))


def _params(n_axes):
    return pltpu.CompilerParams(
        dimension_semantics=("parallel",) * n_axes, vmem_limit_bytes=VMEM_LIMIT)


def _ffn_kernel(x_ref, g_ref, wg_ref, wu_ref, wd_ref, gf_ref, o_ref, a_scr, *, ff_chunk, final_norm):
    x = x_ref[...]
    hb = _rmsnorm(x, g_ref[...]).astype(BF16)
    d_ff = wg_ref.shape[1]
    for c in range(d_ff // ff_chunk):
        sl = slice(c * ff_chunk, (c + 1) * ff_chunk)
        gate = jnp.dot(hb, wg_ref[:, sl], preferred_element_type=F32)
        up = jnp.dot(hb, wu_ref[:, sl], preferred_element_type=F32)
        a_scr[:, sl] = (gate * jax.nn.sigmoid(gate) * up).astype(BF16)
    y = jnp.dot(a_scr[...], wd_ref[...], preferred_element_type=F32)
    x = x + 0.5 * y
    if final_norm:
        x = _rmsnorm(x, gf_ref[...])
    o_ref[...] = x


def _ffn(x, g, wg, wu, wd, gf, *, final_norm, tm=1024, ff_chunk=256):
    T, D = x.shape
    d_ff = wg.shape[1]
    kern = functools.partial(_ffn_kernel, ff_chunk=ff_chunk, final_norm=final_norm)
    return pl.pallas_call(
        kern,
        out_shape=jax.ShapeDtypeStruct((T, D), F32),
        grid=(T // tm,),
        in_specs=[
            pl.BlockSpec((tm, D), lambda i: (i, 0)),
            _const_spec((1, D)),
            _const_spec((D, d_ff)),
            _const_spec((D, d_ff)),
            _const_spec((d_ff, D)),
            _const_spec((1, D)),
        ],
        out_specs=pl.BlockSpec((tm, D), lambda i: (i, 0)),
        scratch_shapes=[pltpu.VMEM((tm, d_ff), BF16)],
        compiler_params=_params(1),
        name="ffn_final" if final_norm else "ffn",
    )(x, g, wg, wu, wd, gf)


def _inproj_kernel(x_ref, g_ref, w_ref, wt_ref, *o_refs, t_rows, t_scales, widths):
    hb = _rmsnorm(x_ref[...], g_ref[...]).astype(BF16)
    nt = (((1,), (1,)), ((), ()))
    off = 0
    for o_ref, rows, scale in zip(o_refs, t_rows, t_scales):
        for c in range(0, rows, 512):
            cw = min(512, rows - c)
            y = lax.dot_general(wt_ref[off + c:off + c + cw, :], hb, nt, preferred_element_type=F32)
            o_ref[c:c + cw, :] = (y * scale if scale != 1.0 else y).astype(BF16)
        off += rows
    off = 0
    for o_ref, width in zip(o_refs[len(t_rows):], widths):
        for c in range(0, width, 512):
            cw = min(512, width - c)
            y = jnp.dot(hb, w_ref[:, off + c:off + c + cw], preferred_element_type=F32)
            o_ref[:, c:c + cw] = y.astype(BF16)
        off += width


def _inproj(x, g, w, wt, t_rows, t_scales, widths, *, tm=1024):
    T, D = x.shape
    kern = functools.partial(_inproj_kernel, t_rows=t_rows, t_scales=t_scales, widths=widths)
    return pl.pallas_call(
        kern,
        out_shape=[jax.ShapeDtypeStruct((rows, T), BF16) for rows in t_rows]
        + [jax.ShapeDtypeStruct((T, wd), BF16) for wd in widths],
        grid=(T // tm,),
        in_specs=[
            pl.BlockSpec((tm, D), lambda i: (i, 0)),
            _const_spec((1, D)),
            _const_spec(w.shape),
            _const_spec(wt.shape),
        ],
        out_specs=[pl.BlockSpec((rows, tm), lambda i: (0, i)) for rows in t_rows]
        + [pl.BlockSpec((tm, wd), lambda i: (i, 0)) for wd in widths],
        compiler_params=_params(1),
        name="in_proj",
    )(x, g, w, wt)


def _split3(x):
    def top_bits(y):
        return lax.bitcast_convert_type(lax.bitcast_convert_type(y, jnp.uint32) & jnp.uint32(0xFFFF0000), F32)
    hi = top_bits(x)
    mid = top_bits(x - hi)
    return [hi, mid, x - hi - mid]


def _da_kernel(slopes_ref, qt_ref, k_ref, vt_ref, dist_ref, lam_ref, sub_ref, o_ref,
               kaug_scr, vaug_scr, knorm_scr, *, lam_init):
    h = pl.program_id(1)
    step = pl.program_id(2)
    tq = DA_TQ
    S = k_ref.shape[0]
    nk = S // tq
    slope2 = slopes_ref[h] * LOG2E

    lp = lam_ref[...]
    lam = (jnp.exp(jnp.sum(lp[0:1] * lp[1:2], axis=-1, keepdims=True))
           - jnp.exp(jnp.sum(lp[2:3] * lp[3:4], axis=-1, keepdims=True)) + lam_init)

    @pl.when(step == 0)
    def _():
        k = k_ref[...]
        lane = lax.broadcasted_iota(jnp.int32, (S, LANES), 1)
        low = lane < HALF
        ones3 = jnp.where((lane % HALF) < 3, 1.0, 0.0).astype(BF16)
        kaug_scr[0] = jnp.where(low, k, ones3)
        kaug_scr[1] = jnp.where(low, ones3, k)
        kf = k.astype(F32)
        ksq = kf * kf
        for c in range(2):
            n2 = jnp.sum(jnp.where(low if c == 0 else ~low, ksq, 0.0), axis=1, keepdims=True)
            knorm_scr[c] = jnp.broadcast_to(jnp.max(n2, axis=0, keepdims=True), (1, LANES))
        vaug_scr[:LANES, :] = vt_ref[...]
        first = lax.broadcasted_iota(jnp.int32, (DA_VAUG_ROWS - LANES, S), 0) == 0
        vaug_scr[LANES:, :] = jnp.where(first, 1.0, 0.0).astype(BF16)

    row = lax.broadcasted_iota(jnp.int32, (HALF, tq), 0)

    def q_halves(sub):
        qt = qt_ref[:, sub * tq:(sub + 1) * tq]
        return qt[:HALF], qt[HALF:]

    def bias_tile(qi, J):
        return dist_ref[J - qi + (nk - 1)] * slope2

    def start(sub, stab):
        halves = q_halves(sub)
        qm = []
        for c in range(2):
            p = _split3(-stab[2 * sub + c])
            spare = jnp.where(row == 0, p[0], jnp.where(row == 1, p[1], jnp.where(row == 2, p[2], 0.0)))
            spare = spare.astype(BF16)
            qm.append(jnp.concatenate([halves[0], spare] if c == 0 else [spare, halves[1]], axis=0))
        return dict(sub=sub, qi=step * DA_SUBBLOCKS + sub, qm=qm, e={}, acc=[None, None])

    def score_piece(st, J):
        keys = slice(J * tq, (J + 1) * tq)
        bias = bias_tile(st["qi"], J)
        for c in range(2):
            s = jnp.dot(kaug_scr[c, keys, :], st["qm"][c], preferred_element_type=F32)
            st["e"][J, c] = jnp.exp2(s - bias).astype(BF16)

    def value_piece(st, J):
        keys = slice(J * tq, (J + 1) * tq)
        for c in range(2):
            pv = jnp.dot(vaug_scr[:, keys], st["e"].pop((J, c)), preferred_element_type=F32)
            st["acc"][c] = pv if J == 0 else st["acc"][c] + pv

    def finish(st):
        (o1, l1), (o2, l2) = ((a[:LANES], a[LANES:LANES + 1]) for a in st["acc"])
        ot = o1 * (1.0 / l1) - o2 * (lam / l2)
        o = _rmsnorm(ot.T, sub_ref[...]) * (1.0 - lam_init)
        o_ref[st["sub"] * tq:(st["sub"] + 1) * tq, :] = o.astype(BF16)

    def stream(stab):
        blocks = {}
        n_pieces = DA_SUBBLOCKS * nk
        for g in range(n_pieces + DA_LAG):
            if g < n_pieces:
                if g % nk == 0:
                    blocks[g // nk] = start(g // nk, stab)
                score_piece(blocks[g // nk], g % nk)
            v = g - DA_LAG
            if v >= 0:
                value_piece(blocks[v // nk], v % nk)
                if v % nk == nk - 1:
                    finish(blocks.pop(v // nk))

    bounds = []
    for sub in range(DA_SUBBLOCKS):
        for c, qh in enumerate(q_halves(sub)):
            qf = qh.astype(F32)
            qn2 = jnp.sum(qf * qf, axis=0, keepdims=True)
            bounds.append(jnp.sqrt(qn2 * knorm_scr[c][:, :1]) * 1.001 + 1e-3)
    stream(bounds)

    worst = bounds[0]
    for b in bounds[1:]:
        worst = jnp.maximum(worst, b)

    @pl.when(jnp.max(worst) > DA_MAX_BOUND)
    def _():
        k = k_ref[...]
        exact = []
        for sub in range(DA_SUBBLOCKS):
            qi = step * DA_SUBBLOCKS + sub
            bias = jnp.concatenate([bias_tile(qi, J) for J in range(nk)], axis=0)
            q1, q2 = q_halves(sub)
            zero = jnp.zeros_like(q1)
            for qm in (jnp.concatenate([q1, zero], axis=0), jnp.concatenate([zero, q2], axis=0)):
                s = jnp.dot(k, qm, preferred_element_type=F32) - bias
                exact.append(jnp.max(s, axis=0, keepdims=True))
        stream(exact)


def _da_attention(qt, k, vt, dist, da_lambda, subnorm, slopes, *, B, S, lam_init):
    T = k.shape[0]
    tq = DA_TQ * DA_SUBBLOCKS
    nq = S // tq
    kern = functools.partial(_da_kernel, lam_init=lam_init)
    return pl.pallas_call(
        kern,
        out_shape=jax.ShapeDtypeStruct((T, DA_HEADS * LANES), BF16),
        grid=(B, DA_HEADS, nq),
        in_specs=[
            pl.BlockSpec(memory_space=pltpu.SMEM),
            pl.BlockSpec((LANES, tq), lambda b, h, i: (h, b * nq + i)),
            pl.BlockSpec((S, LANES), lambda b, h, i: (b, h)),
            pl.BlockSpec((LANES, S), lambda b, h, i: (h, b)),
            _const_spec(dist.shape),
            _const_spec((4, HALF)),
            _const_spec((1, LANES)),
        ],
        out_specs=pl.BlockSpec((tq, LANES), lambda b, h, i: (b * nq + i, h)),
        scratch_shapes=[
            pltpu.VMEM((2, S, LANES), BF16),
            pltpu.VMEM((DA_VAUG_ROWS, S), BF16),
            pltpu.VMEM((2, 1, LANES), F32),
        ],
        compiler_params=pltpu.CompilerParams(
            dimension_semantics=("parallel", "parallel", "arbitrary"), vmem_limit_bytes=VMEM_LIMIT),
        name="da_attn",
    )(slopes, qt, k, vt, dist, da_lambda, subnorm)


def _da_distance_table(S):
    nk = S // DA_TQ
    t = jnp.arange(2 * nk - 1, dtype=jnp.int32)[:, None, None] - (nk - 1)
    r = jnp.arange(DA_TQ, dtype=jnp.int32)[None, :, None]
    m = jnp.arange(DA_TQ, dtype=jnp.int32)[None, None, :]
    return jnp.abs(t * DA_TQ + r - m).astype(F32)


def _sw_kernel(sink_ref, qt_ref, kk_ref, vt_ref, bias_ref, o_ref, kaug_scr, vaug_scr, knorm_scr):
    j = pl.program_id(1)
    ch = pl.program_id(2)
    S = kk_ref.shape[0]
    nb = S // Q_BLOCK
    wide = SW_GROUP * Q_BLOCK
    sink2 = jnp.concatenate(
        [jnp.full((1, Q_BLOCK), sink_ref[j * SW_GROUP + g] * LOG2E, F32) for g in range(SW_GROUP)], axis=1)

    @pl.when(ch == 0)
    def _():
        kk = kk_ref[...]
        lane = lax.broadcasted_iota(jnp.int32, (S, LANES), 1)
        low = lane < HALF
        kaug_scr[...] = jnp.where(low, kk, jnp.where(lane < HALF + 3, 1.0, 0.0).astype(BF16))
        kf = kk.astype(F32)
        n2 = jnp.sum(jnp.where(low, kf * kf, 0.0), axis=1, keepdims=True)
        knorm_scr[...] = jnp.broadcast_to(jnp.max(n2, axis=0, keepdims=True), (1, LANES))
        first = lax.broadcasted_iota(jnp.int32, (SW_VAUG_ROWS - HALF, Q_BLOCK), 0) == 0
        for n in range(nb):
            vaug_scr[n, :HALF, :] = vt_ref[:, n * Q_BLOCK:(n + 1) * Q_BLOCK]
            vaug_scr[n, HALF:, :] = jnp.where(first, 1.0, 0.0).astype(BF16)

    row = lax.broadcasted_iota(jnp.int32, (HALF, wide), 0)

    def block(r):
        n = ch * SW_BLOCKS_PER_STEP + r
        first_blk = jnp.clip(n - 1, 0, nb - 3)
        variant = jnp.where(n == 0, 0, jnp.where(n == nb - 1, 2, 1))
        qt4 = qt_ref[:, r * Q_BLOCK:(r + 1) * Q_BLOCK]
        q_wide = jnp.concatenate([qt4[g * HALF:(g + 1) * HALF] for g in range(SW_GROUP)], axis=1)
        return dict(r=r, first_blk=first_blk, variant=variant, q_wide=q_wide)

    def key_window(blk):
        return kaug_scr[pl.ds(pl.multiple_of(blk["first_blk"] * Q_BLOCK, Q_BLOCK), SW_WIN), :]

    blocks = [block(r) for r in range(SW_BLOCKS_PER_STEP)]

    def scores(blk, m):
        p = _split3(-m)
        spare = jnp.where(row == 0, p[0], jnp.where(row == 1, p[1], jnp.where(row == 2, p[2], 0.0))).astype(BF16)
        qm = jnp.concatenate([blk["q_wide"], spare], axis=0)
        s = jnp.dot(key_window(blk), qm, preferred_element_type=F32)
        return jnp.exp2(s - bias_ref[blk["variant"]]).astype(BF16)

    def values(blk, m, e):
        fb = blk["first_blk"]
        vwin = jnp.concatenate([vaug_scr[fb], vaug_scr[fb + 1], vaug_scr[fb + 2]], axis=1)
        ov = jnp.dot(vwin, e, preferred_element_type=F32)
        denom = ov[HALF:HALF + 1] + jnp.exp2(sink2 - m)
        on = ov[:HALF] * (1.0 / denom)
        rows = slice(blk["r"] * Q_BLOCK, (blk["r"] + 1) * Q_BLOCK)
        for t in range(SW_GROUP // 2):
            pair = jnp.concatenate([on[:, (2 * t) * Q_BLOCK:(2 * t + 1) * Q_BLOCK],
                                    on[:, (2 * t + 1) * Q_BLOCK:(2 * t + 2) * Q_BLOCK]], axis=0)
            o_ref[rows, t * LANES:(t + 1) * LANES] = pair.T.astype(BF16)

    def stream(stabilisers):
        e = {}
        for r in range(SW_BLOCKS_PER_STEP + 1):
            if r < SW_BLOCKS_PER_STEP:
                e[r] = scores(blocks[r], stabilisers[r])
            if r >= 1:
                values(blocks[r - 1], stabilisers[r - 1], e.pop(r - 1))

    bounds = []
    for blk in blocks:
        qf = blk["q_wide"].astype(F32)
        qn2 = jnp.sum(qf * qf, axis=0, keepdims=True)
        bounds.append(jnp.sqrt(qn2 * knorm_scr[:, :1]) * 1.001 + 1e-3)
    stream([jnp.maximum(b, sink2) for b in bounds])

    worst = bounds[0]
    for b in bounds[1:]:
        worst = jnp.maximum(worst, b)

    @pl.when(jnp.max(worst) > DA_MAX_BOUND)
    def _():
        exact = []
        for blk in blocks:
            qm = jnp.concatenate([blk["q_wide"], jnp.zeros_like(blk["q_wide"])], axis=0)
            s = jnp.dot(key_window(blk), qm, preferred_element_type=F32) - bias_ref[blk["variant"]]
            exact.append(jnp.maximum(jnp.max(s, axis=0, keepdims=True), sink2))
        stream(exact)


def _sw_attention(qt, kk, vt, bias, sink, *, B, S):
    T = kk.shape[0]
    tq = SW_BLOCKS_PER_STEP * Q_BLOCK
    nq = S // tq
    nb = S // Q_BLOCK
    gw = SW_GROUP * HALF
    return pl.pallas_call(
        _sw_kernel,
        out_shape=jax.ShapeDtypeStruct((T, SW_Q_HEADS * HALF), BF16),
        grid=(B, SW_KV_HEADS, nq),
        in_specs=[
            pl.BlockSpec(memory_space=pltpu.SMEM),
            pl.BlockSpec((gw, tq), lambda b, j, n: (j, b * nq + n)),
            pl.BlockSpec((S, LANES), lambda b, j, n: (b, j)),
            pl.BlockSpec((HALF, S), lambda b, j, n: (j, b)),
            pl.BlockSpec((None,) + bias.shape[1:], lambda b, j, n: (j, 0, 0, 0)),
        ],
        out_specs=pl.BlockSpec((tq, gw), lambda b, j, n: (b * nq + n, j)),
        scratch_shapes=[
            pltpu.VMEM((S, LANES), BF16),
            pltpu.VMEM((nb, SW_VAUG_ROWS, Q_BLOCK), BF16),
            pltpu.VMEM((1, LANES), F32),
        ],
        compiler_params=pltpu.CompilerParams(
            dimension_semantics=("parallel", "parallel", "arbitrary"), vmem_limit_bytes=VMEM_LIMIT),
        name="sw_attn",
    )(sink, qt, kk, vt, bias)


def _sw_bias_table(slopes):
    shift = jnp.arange(3, dtype=jnp.int32)[:, None, None] * Q_BLOCK
    c = jnp.arange(SW_WIN, dtype=jnp.int32)[None, :, None]
    r = jnp.arange(Q_BLOCK, dtype=jnp.int32)[None, None, :]
    d = jnp.abs(shift + r - c)
    dist = jnp.where(d <= WINDOW, d.astype(F32), MASKED_DIST)
    s2 = (slopes * LOG2E).reshape(SW_KV_HEADS, 1, 1, SW_GROUP, 1)
    table = jnp.minimum(dist[None, :, :, None, :] * s2, MASKED_DIST)
    return table.reshape(SW_KV_HEADS, 3, SW_WIN, SW_GROUP * Q_BLOCK)


def _merge_kernel(x_ref, g_ref, wgate_ref, bg_ref, oda_ref, osw_ref, pa_ref, pb_ref, wo_ref, o_ref):
    x = x_ref[...]
    D = x.shape[1]
    hb = _rmsnorm(x, g_ref[...]).astype(BF16)
    gates = jnp.dot(hb, wgate_ref[...], preferred_element_type=F32) + bg_ref[...]
    gates = jax.nn.sigmoid(gates)
    ya = jnp.dot(oda_ref[...], pa_ref[...], preferred_element_type=F32)
    yb = jnp.dot(osw_ref[...], pb_ref[...], preferred_element_type=F32)
    merged = gates[:, :D] * ya + gates[:, D:] * yb
    o_ref[...] = x + jnp.dot(merged.astype(BF16), wo_ref[...], preferred_element_type=F32)


def _merge(x, g, wgate, bg, oda, osw, pa, pb, wo, *, tm=1024):
    T, D = x.shape
    row = lambda i: (i, 0)
    return pl.pallas_call(
        _merge_kernel,
        out_shape=jax.ShapeDtypeStruct((T, D), F32),
        grid=(T // tm,),
        in_specs=[
            pl.BlockSpec((tm, D), row),
            _const_spec((1, D)),
            _const_spec((D, 2 * D)),
            _const_spec((1, 2 * D)),
            pl.BlockSpec((tm, D), row),
            pl.BlockSpec((tm, D), row),
            _const_spec((D, D)),
            _const_spec((D, D)),
            _const_spec((D, D)),
        ],
        out_specs=pl.BlockSpec((tm, D), row),
        compiler_params=_params(1),
        name="merge",
    )(x, g, wgate, bg, oda, osw, pa, pb, wo)


def _alibi_slopes(n):
    return 2.0 ** (-8.0 * jnp.arange(1, n + 1, dtype=F32) / n)


def kernel(x, norm_ffn1, ffn1_gate, ffn1_up, ffn1_down, norm_mix, w_in, b_gate, da_lambda, da_subnorm, swa_sink, w_proj_da, w_proj_swa, w_out, norm_ffn2, ffn2_gate, ffn2_up, ffn2_down, norm_final):
    B, S, D = x.shape
    depth = norm_ffn1.shape[0]
    assert depth >= 1, "the final RMSNorm is fused into the last layer's second FFN"
    T = B * S
    xt = x.reshape(T, D)
    bf = lambda a: a.astype(BF16)
    row = lambda a: a.reshape(1, -1)
    gf = row(norm_final)

    da_w = DA_HEADS * LANES
    swq_w = SW_Q_HEADS * HALF
    swkv_w = SW_KV_HEADS * HALF
    o_k, o_v = da_w, 2 * da_w
    o_sq = o_v + da_w
    o_sk = o_sq + swq_w
    o_sv = o_sk + swkv_w
    o_gate = o_sv + swkv_w

    def dup_heads(w):
        w = w.reshape(D, SW_KV_HEADS, 1, HALF)
        return jnp.broadcast_to(w, (D, SW_KV_HEADS, 2, HALF)).reshape(D, SW_KV_HEADS * LANES)

    t_rows = (da_w, da_w, swq_w, swkv_w)
    qscale = HALF ** -0.5 * LOG2E
    t_scales = (qscale, 1.0, qscale, 1.0)
    widths = (da_w, SW_KV_HEADS * LANES)
    da_dist = _da_distance_table(S)
    sw_bias = _sw_bias_table(_alibi_slopes(SW_Q_HEADS))

    for l in range(depth):
        lam_init = 0.8 - 0.6 * math.exp(-0.3 * l)
        wl = w_in[l]
        w_cols = bf(jnp.concatenate([wl[:, o_k:o_v], dup_heads(wl[:, o_sk:o_sv])], axis=1))
        w_t = bf(jnp.concatenate([wl[:, :o_k], wl[:, o_v:o_sk], wl[:, o_sv:o_gate]], axis=1).T)

        xt = _ffn(xt, row(norm_ffn1[l]), bf(ffn1_gate[l]), bf(ffn1_up[l]), bf(ffn1_down[l]), gf,
                  final_norm=False)
        da_qt, da_vt, sw_qt, sw_vt, da_k, sw_kk = _inproj(
            xt, row(norm_mix[l]), w_cols, w_t, t_rows, t_scales, widths)
        o_da = _da_attention(da_qt, da_k, da_vt, da_dist, da_lambda[l], row(da_subnorm[l]),
                             _alibi_slopes(DA_HEADS), B=B, S=S, lam_init=lam_init)
        o_sw = _sw_attention(sw_qt, sw_kk, sw_vt, sw_bias, swa_sink[l], B=B, S=S)
        xt = _merge(xt, row(norm_mix[l]), bf(wl[:, o_gate:]), row(b_gate[l]), o_da, o_sw,
                    bf(w_proj_da[l]), bf(w_proj_swa[l]), bf(w_out[l]))
        xt = _ffn(xt, row(norm_ffn2[l]), bf(ffn2_gate[l]), bf(ffn2_up[l]), bf(ffn2_down[l]), gf,
                  final_norm=(l == depth - 1))
    return xt.reshape(B, S, D)
```

```python
import functools
import math

import jax
import jax.numpy as jnp
from jax import lax
from jax.experimental import pallas as pl
from jax.experimental.pallas import tpu as pltpu

F32 = jnp.float32
BF16 = jnp.bfloat16

RMS_EPS = 1e-6
LOG2E = math.log2(math.e)
LANES = 128
HALF = 64
VMEM_LIMIT = 56 * 1024 * 1024
MASKED_DIST = 1e30

DA_HEADS = 8
DA_TQ = 256
DA_SUBBLOCKS = 8
DA_VAUG_ROWS = LANES + 16
DA_LAG = 2
DA_MAX_BOUND = 40.0
SW_Q_HEADS = 16
SW_KV_HEADS = 4
SW_GROUP = SW_Q_HEADS // SW_KV_HEADS
SW_BLOCKS_PER_STEP = 16
WINDOW = 128
Q_BLOCK = 128
SW_WIN = 3 * Q_BLOCK
SW_VAUG_ROWS = HALF + 16


def _rmsnorm(x, g):
    return x * lax.rsqrt(jnp.mean(x * x, axis=-1, keepdims=True) + RMS_EPS) * g


def _const_spec(shape):
    return pl.BlockSpec(shape, lambda *_: (0,) * len(shape), pipeline_mode=pl.Buffered(1))


def _params(n_axes):
    return pltpu.CompilerParams(
        dimension_semantics=("parallel",) * n_axes, vmem_limit_bytes=VMEM_LIMIT)


def _ffn_kernel(x_ref, g_ref, wg_ref, wu_ref, wd_ref, gf_ref, o_ref, a_scr, *, ff_chunk, final_norm):
    x = x_ref[...]
    hb = _rmsnorm(x, g_ref[...]).astype(BF16)
    d_ff = wg_ref.shape[1]
    for c in range(d_ff // ff_chunk):
        sl = slice(c * ff_chunk, (c + 1) * ff_chunk)
        gate = jnp.dot(hb, wg_ref[:, sl], preferred_element_type=F32)
        up = jnp.dot(hb, wu_ref[:, sl], preferred_element_type=F32)
        a_scr[:, sl] = (gate * jax.nn.sigmoid(gate) * up).astype(BF16)
    y = jnp.dot(a_scr[...], wd_ref[...], preferred_element_type=F32)
    x = x + 0.5 * y
    if final_norm:
        x = _rmsnorm(x, gf_ref[...])
    o_ref[...] = x


def _ffn(x, g, wg, wu, wd, gf, *, final_norm, tm=1024, ff_chunk=256):
    T, D = x.shape
    d_ff = wg.shape[1]
    kern = functools.partial(_ffn_kernel, ff_chunk=ff_chunk, final_norm=final_norm)
    return pl.pallas_call(
        kern,
        out_shape=jax.ShapeDtypeStruct((T, D), F32),
        grid=(T // tm,),
        in_specs=[
            pl.BlockSpec((tm, D), lambda i: (i, 0)),
            _const_spec((1, D)),
            _const_spec((D, d_ff)),
            _const_spec((D, d_ff)),
            _const_spec((d_ff, D)),
            _const_spec((1, D)),
        ],
        out_specs=pl.BlockSpec((tm, D), lambda i: (i, 0)),
        scratch_shapes=[pltpu.VMEM((tm, d_ff), BF16)],
        compiler_params=_params(1),
        name="ffn_final" if final_norm else "ffn",
    )(x, g, wg, wu, wd, gf)


def _inproj_kernel(x_ref, g_ref, w_ref, wt_ref, *o_refs, t_rows, t_scales, widths):
    hb = _rmsnorm(x_ref[...], g_ref[...]).astype(BF16)
    nt = (((1,), (1,)), ((), ()))
    off = 0
    for o_ref, rows, scale in zip(o_refs, t_rows, t_scales):
        for c in range(0, rows, 512):
            cw = min(512, rows - c)
            y = lax.dot_general(wt_ref[off + c:off + c + cw, :], hb, nt, preferred_element_type=F32)
            o_ref[c:c + cw, :] = (y * scale if scale != 1.0 else y).astype(BF16)
        off += rows
    off = 0
    for o_ref, width in zip(o_refs[len(t_rows):], widths):
        for c in range(0, width, 512):
            cw = min(512, width - c)
            y = jnp.dot(hb, w_ref[:, off + c:off + c + cw], preferred_element_type=F32)
            o_ref[:, c:c + cw] = y.astype(BF16)
        off += width


def _inproj(x, g, w, wt, t_rows, t_scales, widths, *, tm=1024):
    T, D = x.shape
    kern = functools.partial(_inproj_kernel, t_rows=t_rows, t_scales=t_scales, widths=widths)
    return pl.pallas_call(
        kern,
        out_shape=[jax.ShapeDtypeStruct((rows, T), BF16) for rows in t_rows]
        + [jax.ShapeDtypeStruct((T, wd), BF16) for wd in widths],
        grid=(T // tm,),
        in_specs=[
            pl.BlockSpec((tm, D), lambda i: (i, 0)),
            _const_spec((1, D)),
            _const_spec(w.shape),
            _const_spec(wt.shape),
        ],
        out_specs=[pl.BlockSpec((rows, tm), lambda i: (0, i)) for rows in t_rows]
        + [pl.BlockSpec((tm, wd), lambda i: (i, 0)) for wd in widths],
        compiler_params=_params(1),
        name="in_proj",
    )(x, g, w, wt)


def _split3(x):
    def top_bits(y):
        return lax.bitcast_convert_type(lax.bitcast_convert_type(y, jnp.uint32) & jnp.uint32(0xFFFF0000), F32)
    hi = top_bits(x)
    mid = top_bits(x - hi)
    return [hi, mid, x - hi - mid]


def _da_kernel(slopes_ref, qt_ref, k_ref, vt_ref, dist_ref, lam_ref, sub_ref, o_ref,
               kaug_scr, vaug_scr, knorm_scr, *, lam_init):
    h = pl.program_id(1)
    step = pl.program_id(2)
    tq = DA_TQ
    S = k_ref.shape[0]
    nk = S // tq
    slope2 = slopes_ref[h] * LOG2E

    lp = lam_ref[...]
    lam = (jnp.exp(jnp.sum(lp[0:1] * lp[1:2], axis=-1, keepdims=True))
           - jnp.exp(jnp.sum(lp[2:3] * lp[3:4], axis=-1, keepdims=True)) + lam_init)

    @pl.when(step == 0)
    def _():
        k = k_ref[...]
        lane = lax.broadcasted_iota(jnp.int32, (S, LANES), 1)
        low = lane < HALF
        ones3 = jnp.where((lane % HALF) < 3, 1.0, 0.0).astype(BF16)
        kaug_scr[0] = jnp.where(low, k, ones3)
        kaug_scr[1] = jnp.where(low, ones3, k)
        kf = k.astype(F32)
        ksq = kf * kf
        for c in range(2):
            n2 = jnp.sum(jnp.where(low if c == 0 else ~low, ksq, 0.0), axis=1, keepdims=True)
            knorm_scr[c] = jnp.broadcast_to(jnp.max(n2, axis=0, keepdims=True), (1, LANES))
        vaug_scr[:LANES, :] = vt_ref[...]
        first = lax.broadcasted_iota(jnp.int32, (DA_VAUG_ROWS - LANES, S), 0) == 0
        vaug_scr[LANES:, :] = jnp.where(first, 1.0, 0.0).astype(BF16)

    row = lax.broadcasted_iota(jnp.int32, (HALF, tq), 0)

    def q_halves(sub):
        qt = qt_ref[:, sub * tq:(sub + 1) * tq]
        return qt[:HALF], qt[HALF:]

    def bias_tile(qi, J):
        return dist_ref[J - qi + (nk - 1)] * slope2

    def start(sub, stab):
        halves = q_halves(sub)
        qm = []
        for c in range(2):
            p = _split3(-stab[2 * sub + c])
            spare = jnp.where(row == 0, p[0], jnp.where(row == 1, p[1], jnp.where(row == 2, p[2], 0.0)))
            spare = spare.astype(BF16)
            qm.append(jnp.concatenate([halves[0], spare] if c == 0 else [spare, halves[1]], axis=0))
        return dict(sub=sub, qi=step * DA_SUBBLOCKS + sub, qm=qm, e={}, acc=[None, None])

    def score_piece(st, J):
        keys = slice(J * tq, (J + 1) * tq)
        bias = bias_tile(st["qi"], J)
        for c in range(2):
            s = jnp.dot(kaug_scr[c, keys, :], st["qm"][c], preferred_element_type=F32)
            st["e"][J, c] = jnp.exp2(s - bias).astype(BF16)

    def value_piece(st, J):
        keys = slice(J * tq, (J + 1) * tq)
        for c in range(2):
            pv = jnp.dot(vaug_scr[:, keys], st["e"].pop((J, c)), preferred_element_type=F32)
            st["acc"][c] = pv if J == 0 else st["acc"][c] + pv

    def finish(st):
        (o1, l1), (o2, l2) = ((a[:LANES], a[LANES:LANES + 1]) for a in st["acc"])
        ot = o1 * (1.0 / l1) - o2 * (lam / l2)
        o = _rmsnorm(ot.T, sub_ref[...]) * (1.0 - lam_init)
        o_ref[st["sub"] * tq:(st["sub"] + 1) * tq, :] = o.astype(BF16)

    def stream(stab):
        blocks = {}
        n_pieces = DA_SUBBLOCKS * nk
        for g in range(n_pieces + DA_LAG):
            if g < n_pieces:
                if g % nk == 0:
                    blocks[g // nk] = start(g // nk, stab)
                score_piece(blocks[g // nk], g % nk)
            v = g - DA_LAG
            if v >= 0:
                value_piece(blocks[v // nk], v % nk)
                if v % nk == nk - 1:
                    finish(blocks.pop(v // nk))

    bounds = []
    for sub in range(DA_SUBBLOCKS):
        for c, qh in enumerate(q_halves(sub)):
            qf = qh.astype(F32)
            qn2 = jnp.sum(qf * qf, axis=0, keepdims=True)
            bounds.append(jnp.sqrt(qn2 * knorm_scr[c][:, :1]) * 1.001 + 1e-3)
    stream(bounds)

    worst = bounds[0]
    for b in bounds[1:]:
        worst = jnp.maximum(worst, b)

    @pl.when(jnp.max(worst) > DA_MAX_BOUND)
    def _():
        k = k_ref[...]
        exact = []
        for sub in range(DA_SUBBLOCKS):
            qi = step * DA_SUBBLOCKS + sub
            bias = jnp.concatenate([bias_tile(qi, J) for J in range(nk)], axis=0)
            q1, q2 = q_halves(sub)
            zero = jnp.zeros_like(q1)
            for qm in (jnp.concatenate([q1, zero], axis=0), jnp.concatenate([zero, q2], axis=0)):
                s = jnp.dot(k, qm, preferred_element_type=F32) - bias
                exact.append(jnp.max(s, axis=0, keepdims=True))
        stream(exact)


def _da_attention(qt, k, vt, dist, da_lambda, subnorm, slopes, *, B, S, lam_init):
    T = k.shape[0]
    tq = DA_TQ * DA_SUBBLOCKS
    nq = S // tq
    kern = functools.partial(_da_kernel, lam_init=lam_init)
    return pl.pallas_call(
        kern,
        out_shape=jax.ShapeDtypeStruct((T, DA_HEADS * LANES), BF16),
        grid=(B, DA_HEADS, nq),
        in_specs=[
            pl.BlockSpec(memory_space=pltpu.SMEM),
            pl.BlockSpec((LANES, tq), lambda b, h, i: (h, b * nq + i)),
            pl.BlockSpec((S, LANES), lambda b, h, i: (b, h)),
            pl.BlockSpec((LANES, S), lambda b, h, i: (h, b)),
            _const_spec(dist.shape),
            _const_spec((4, HALF)),
            _const_spec((1, LANES)),
        ],
        out_specs=pl.BlockSpec((tq, LANES), lambda b, h, i: (b * nq + i, h)),
        scratch_shapes=[
            pltpu.VMEM((2, S, LANES), BF16),
            pltpu.VMEM((DA_VAUG_ROWS, S), BF16),
            pltpu.VMEM((2, 1, LANES), F32),
        ],
        compiler_params=pltpu.CompilerParams(
            dimension_semantics=("parallel", "parallel", "arbitrary"), vmem_limit_bytes=VMEM_LIMIT),
        name="da_attn",
    )(slopes, qt, k, vt, dist, da_lambda, subnorm)


def _da_distance_table(S):
    nk = S // DA_TQ
    t = jnp.arange(2 * nk - 1, dtype=jnp.int32)[:, None, None] - (nk - 1)
    r = jnp.arange(DA_TQ, dtype=jnp.int32)[None, :, None]
    m = jnp.arange(DA_TQ, dtype=jnp.int32)[None, None, :]
    return jnp.abs(t * DA_TQ + r - m).astype(F32)


def _sw_kernel(sink_ref, qt_ref, kk_ref, vt_ref, bias_ref, o_ref, kaug_scr, vaug_scr, knorm_scr):
    j = pl.program_id(1)
    ch = pl.program_id(2)
    S = kk_ref.shape[0]
    nb = S // Q_BLOCK
    wide = SW_GROUP * Q_BLOCK
    sink2 = jnp.concatenate(
        [jnp.full((1, Q_BLOCK), sink_ref[j * SW_GROUP + g] * LOG2E, F32) for g in range(SW_GROUP)], axis=1)

    @pl.when(ch == 0)
    def _():
        kk = kk_ref[...]
        lane = lax.broadcasted_iota(jnp.int32, (S, LANES), 1)
        low = lane < HALF
        kaug_scr[...] = jnp.where(low, kk, jnp.where(lane < HALF + 3, 1.0, 0.0).astype(BF16))
        kf = kk.astype(F32)
        n2 = jnp.sum(jnp.where(low, kf * kf, 0.0), axis=1, keepdims=True)
        knorm_scr[...] = jnp.broadcast_to(jnp.max(n2, axis=0, keepdims=True), (1, LANES))
        first = lax.broadcasted_iota(jnp.int32, (SW_VAUG_ROWS - HALF, Q_BLOCK), 0) == 0
        for n in range(nb):
            vaug_scr[n, :HALF, :] = vt_ref[:, n * Q_BLOCK:(n + 1) * Q_BLOCK]
            vaug_scr[n, HALF:, :] = jnp.where(first, 1.0, 0.0).astype(BF16)

    row = lax.broadcasted_iota(jnp.int32, (HALF, wide), 0)

    def block(r):
        n = ch * SW_BLOCKS_PER_STEP + r
        first_blk = jnp.clip(n - 1, 0, nb - 3)
        variant = jnp.where(n == 0, 0, jnp.where(n == nb - 1, 2, 1))
        qt4 = qt_ref[:, r * Q_BLOCK:(r + 1) * Q_BLOCK]
        q_wide = jnp.concatenate([qt4[g * HALF:(g + 1) * HALF] for g in range(SW_GROUP)], axis=1)
        return dict(r=r, first_blk=first_blk, variant=variant, q_wide=q_wide)

    def key_window(blk):
        return kaug_scr[pl.ds(pl.multiple_of(blk["first_blk"] * Q_BLOCK, Q_BLOCK), SW_WIN), :]

    blocks = [block(r) for r in range(SW_BLOCKS_PER_STEP)]

    def scores(blk, m):
        p = _split3(-m)
        spare = jnp.where(row == 0, p[0], jnp.where(row == 1, p[1], jnp.where(row == 2, p[2], 0.0))).astype(BF16)
        qm = jnp.concatenate([blk["q_wide"], spare], axis=0)
        s = jnp.dot(key_window(blk), qm, preferred_element_type=F32)
        return jnp.exp2(s - bias_ref[blk["variant"]]).astype(BF16)

    def values(blk, m, e):
        fb = blk["first_blk"]
        vwin = jnp.concatenate([vaug_scr[fb], vaug_scr[fb + 1], vaug_scr[fb + 2]], axis=1)
        ov = jnp.dot(vwin, e, preferred_element_type=F32)
        denom = ov[HALF:HALF + 1] + jnp.exp2(sink2 - m)
        on = ov[:HALF] * (1.0 / denom)
        rows = slice(blk["r"] * Q_BLOCK, (blk["r"] + 1) * Q_BLOCK)
        for t in range(SW_GROUP // 2):
            pair = jnp.concatenate([on[:, (2 * t) * Q_BLOCK:(2 * t + 1) * Q_BLOCK],
                                    on[:, (2 * t + 1) * Q_BLOCK:(2 * t + 2) * Q_BLOCK]], axis=0)
            o_ref[rows, t * LANES:(t + 1) * LANES] = pair.T.astype(BF16)

    def stream(stabilisers):
        e = {}
        for r in range(SW_BLOCKS_PER_STEP + 1):
            if r < SW_BLOCKS_PER_STEP:
                e[r] = scores(blocks[r], stabilisers[r])
            if r >= 1:
                values(blocks[r - 1], stabilisers[r - 1], e.pop(r - 1))

    bounds = []
    for blk in blocks:
        qf = blk["q_wide"].astype(F32)
        qn2 = jnp.sum(qf * qf, axis=0, keepdims=True)
        bounds.append(jnp.sqrt(qn2 * knorm_scr[:, :1]) * 1.001 + 1e-3)
    stream([jnp.maximum(b, sink2) for b in bounds])

    worst = bounds[0]
    for b in bounds[1:]:
        worst = jnp.maximum(worst, b)

    @pl.when(jnp.max(worst) > DA_MAX_BOUND)
    def _():
        exact = []
        for blk in blocks:
            qm = jnp.concatenate([blk["q_wide"], jnp.zeros_like(blk["q_wide"])], axis=0)
            s = jnp.dot(key_window(blk), qm, preferred_element_type=F32) - bias_ref[blk["variant"]]
            exact.append(jnp.maximum(jnp.max(s, axis=0, keepdims=True), sink2))
        stream(exact)


def _sw_attention(qt, kk, vt, bias, sink, *, B, S):
    T = kk.shape[0]
    tq = SW_BLOCKS_PER_STEP * Q_BLOCK
    nq = S // tq
    nb = S // Q_BLOCK
    gw = SW_GROUP * HALF
    return pl.pallas_call(
        _sw_kernel,
        out_shape=jax.ShapeDtypeStruct((T, SW_Q_HEADS * HALF), BF16),
        grid=(B, SW_KV_HEADS, nq),
        in_specs=[
            pl.BlockSpec(memory_space=pltpu.SMEM),
            pl.BlockSpec((gw, tq), lambda b, j, n: (j, b * nq + n)),
            pl.BlockSpec((S, LANES), lambda b, j, n: (b, j)),
            pl.BlockSpec((HALF, S), lambda b, j, n: (j, b)),
            pl.BlockSpec((None,) + bias.shape[1:], lambda b, j, n: (j, 0, 0, 0)),
        ],
        out_specs=pl.BlockSpec((tq, gw), lambda b, j, n: (b * nq + n, j)),
        scratch_shapes=[
            pltpu.VMEM((S, LANES), BF16),
            pltpu.VMEM((nb, SW_VAUG_ROWS, Q_BLOCK), BF16),
            pltpu.VMEM((1, LANES), F32),
        ],
        compiler_params=pltpu.CompilerParams(
            dimension_semantics=("parallel", "parallel", "arbitrary"), vmem_limit_bytes=VMEM_LIMIT),
        name="sw_attn",
    )(sink, qt, kk, vt, bias)


def _sw_bias_table(slopes):
    shift = jnp.arange(3, dtype=jnp.int32)[:, None, None] * Q_BLOCK
    c = jnp.arange(SW_WIN, dtype=jnp.int32)[None, :, None]
    r = jnp.arange(Q_BLOCK, dtype=jnp.int32)[None, None, :]
    d = jnp.abs(shift + r - c)
    dist = jnp.where(d <= WINDOW, d.astype(F32), MASKED_DIST)
    s2 = (slopes * LOG2E).reshape(SW_KV_HEADS, 1, 1, SW_GROUP, 1)
    table = jnp.minimum(dist[None, :, :, None, :] * s2, MASKED_DIST)
    return table.reshape(SW_KV_HEADS, 3, SW_WIN, SW_GROUP * Q_BLOCK)


def _merge_kernel(x_ref, g_ref, wgate_ref, bg_ref, oda_ref, osw_ref, pa_ref, pb_ref, wo_ref, o_ref):
    x = x_ref[...]
    D = x.shape[1]
    hb = _rmsnorm(x, g_ref[...]).astype(BF16)
    gates = jnp.dot(hb, wgate_ref[...], preferred_element_type=F32) + bg_ref[...]
    gates = jax.nn.sigmoid(gates)
    ya = jnp.dot(oda_ref[...], pa_ref[...], preferred_element_type=F32)
    yb = jnp.dot(osw_ref[...], pb_ref[...], preferred_element_type=F32)
    merged = gates[:, :D] * ya + gates[:, D:] * yb
    o_ref[...] = x + jnp.dot(merged.astype(BF16), wo_ref[...], preferred_element_type=F32)


def _merge(x, g, wgate, bg, oda, osw, pa, pb, wo, *, tm=1024):
    T, D = x.shape
    row = lambda i: (i, 0)
    return pl.pallas_call(
        _merge_kernel,
        out_shape=jax.ShapeDtypeStruct((T, D), F32),
        grid=(T // tm,),
        in_specs=[
            pl.BlockSpec((tm, D), row),
            _const_spec((1, D)),
            _const_spec((D, 2 * D)),
            _const_spec((1, 2 * D)),
            pl.BlockSpec((tm, D), row),
            pl.BlockSpec((tm, D), row),
            _const_spec((D, D)),
            _const_spec((D, D)),
            _const_spec((D, D)),
        ],
        out_specs=pl.BlockSpec((tm, D), row),
        compiler_params=_params(1),
        name="merge",
    )(x, g, wgate, bg, oda, osw, pa, pb, wo)


def _merge_ffn_kernel(x_ref, gm_ref, wgate_ref, bg_ref, oda_ref, osw_ref, pa_ref, pb_ref, wo_ref,
                      g_ref, wg_ref, wu_ref, wd_ref, gf_ref, o_ref, a_scr, *, ff_chunk, final_norm):
    x = x_ref[...]
    D = x.shape[1]
    hb = _rmsnorm(x, gm_ref[...]).astype(BF16)
    gates = jax.nn.sigmoid(jnp.dot(hb, wgate_ref[...], preferred_element_type=F32) + bg_ref[...])
    ya = jnp.dot(oda_ref[...], pa_ref[...], preferred_element_type=F32)
    yb = jnp.dot(osw_ref[...], pb_ref[...], preferred_element_type=F32)
    merged = gates[:, :D] * ya + gates[:, D:] * yb
    x = x + jnp.dot(merged.astype(BF16), wo_ref[...], preferred_element_type=F32)
    hb = _rmsnorm(x, g_ref[...]).astype(BF16)
    d_ff = wg_ref.shape[1]
    for c in range(d_ff // ff_chunk):
        sl = slice(c * ff_chunk, (c + 1) * ff_chunk)
        gate = jnp.dot(hb, wg_ref[:, sl], preferred_element_type=F32)
        up = jnp.dot(hb, wu_ref[:, sl], preferred_element_type=F32)
        a_scr[:, sl] = (gate * jax.nn.sigmoid(gate) * up).astype(BF16)
    x = x + 0.5 * jnp.dot(a_scr[...], wd_ref[...], preferred_element_type=F32)
    if final_norm:
        x = _rmsnorm(x, gf_ref[...])
    o_ref[...] = x


def _merge_ffn(x, gm, wgate, bg, oda, osw, pa, pb, wo, g, wg, wu, wd, gf, *, final_norm, tm=512, ff_chunk=256):
    T, D = x.shape
    d_ff = wg.shape[1]
    row = lambda i: (i, 0)
    kern = functools.partial(_merge_ffn_kernel, ff_chunk=ff_chunk, final_norm=final_norm)
    return pl.pallas_call(
        kern,
        out_shape=jax.ShapeDtypeStruct((T, D), F32),
        grid=(T // tm,),
        in_specs=[
            pl.BlockSpec((tm, D), row),
            _const_spec((1, D)),
            _const_spec((D, 2 * D)),
            _const_spec((1, 2 * D)),
            pl.BlockSpec((tm, D), row),
            pl.BlockSpec((tm, D), row),
            _const_spec((D, D)),
            _const_spec((D, D)),
            _const_spec((D, D)),
            _const_spec((1, D)),
            _const_spec((D, d_ff)),
            _const_spec((D, d_ff)),
            _const_spec((d_ff, D)),
            _const_spec((1, D)),
        ],
        out_specs=pl.BlockSpec((tm, D), row),
        scratch_shapes=[pltpu.VMEM((tm, d_ff), BF16)],
        compiler_params=_params(1),
        name="merge_ffn",
    )(x, gm, wgate, bg, oda, osw, pa, pb, wo, g, wg, wu, wd, gf)


def _alibi_slopes(n):
    return 2.0 ** (-8.0 * jnp.arange(1, n + 1, dtype=F32) / n)


def kernel(x, norm_ffn1, ffn1_gate, ffn1_up, ffn1_down, norm_mix, w_in, b_gate, da_lambda, da_subnorm, swa_sink, w_proj_da, w_proj_swa, w_out, norm_ffn2, ffn2_gate, ffn2_up, ffn2_down, norm_final):
    B, S, D = x.shape
    depth = norm_ffn1.shape[0]
    assert depth >= 1, "the final RMSNorm is fused into the last layer's second FFN"
    T = B * S
    xt = x.reshape(T, D)
    bf = lambda a: a.astype(BF16)
    row = lambda a: a.reshape(1, -1)
    gf = row(norm_final)

    da_w = DA_HEADS * LANES
    swq_w = SW_Q_HEADS * HALF
    swkv_w = SW_KV_HEADS * HALF
    o_k, o_v = da_w, 2 * da_w
    o_sq = o_v + da_w
    o_sk = o_sq + swq_w
    o_sv = o_sk + swkv_w
    o_gate = o_sv + swkv_w

    def dup_heads(w):
        w = w.reshape(D, SW_KV_HEADS, 1, HALF)
        return jnp.broadcast_to(w, (D, SW_KV_HEADS, 2, HALF)).reshape(D, SW_KV_HEADS * LANES)

    t_rows = (da_w, da_w, swq_w, swkv_w)
    qscale = HALF ** -0.5 * LOG2E
    t_scales = (qscale, 1.0, qscale, 1.0)
    widths = (da_w, SW_KV_HEADS * LANES)
    da_dist = _da_distance_table(S)
    sw_bias = _sw_bias_table(_alibi_slopes(SW_Q_HEADS))

    for l in range(depth):
        lam_init = 0.8 - 0.6 * math.exp(-0.3 * l)
        wl = w_in[l]
        w_cols = bf(jnp.concatenate([wl[:, o_k:o_v], dup_heads(wl[:, o_sk:o_sv])], axis=1))
        w_t = bf(jnp.concatenate([wl[:, :o_k], wl[:, o_v:o_sk], wl[:, o_sv:o_gate]], axis=1).T)

        xt = _ffn(xt, row(norm_ffn1[l]), bf(ffn1_gate[l]), bf(ffn1_up[l]), bf(ffn1_down[l]), gf,
                  final_norm=False)
        da_qt, da_vt, sw_qt, sw_vt, da_k, sw_kk = _inproj(
            xt, row(norm_mix[l]), w_cols, w_t, t_rows, t_scales, widths)
        o_da = _da_attention(da_qt, da_k, da_vt, da_dist, da_lambda[l], row(da_subnorm[l]),
                             _alibi_slopes(DA_HEADS), B=B, S=S, lam_init=lam_init)
        o_sw = _sw_attention(sw_qt, sw_kk, sw_vt, sw_bias, swa_sink[l], B=B, S=S)
        xt = _merge_ffn(xt, row(norm_mix[l]), bf(wl[:, o_gate:]), row(b_gate[l]), o_da, o_sw,
                        bf(w_proj_da[l]), bf(w_proj_swa[l]), bf(w_out[l]),
                        row(norm_ffn2[l]), bf(ffn2_gate[l]), bf(ffn2_up[l]), bf(ffn2_down[l]), gf,
                        final_norm=(l == depth - 1))
    return xt.reshape(B, S, D)
```
